```python
import jax, jax.numpy as jnp
from jax import lax
import numpy as np

D_MODEL = 4096
BATCH = 8
SEQ = 2048
DEPTH = 1
DEC_BATCH = 1
DEC_SEQ = 8192
PAST_LEN = 128

HEAD_DIM = 128
N_HEADS_TOTAL = D_MODEL // HEAD_DIM
A_Q_HEADS = N_HEADS_TOTAL // 2
A_KV_HEADS = A_Q_HEADS // 4
B_HEADS = N_HEADS_TOTAL - A_Q_HEADS
A_Q_W = A_Q_HEADS * HEAD_DIM
A_KV_W = A_KV_HEADS * HEAD_DIM
B_W = B_HEADS * HEAD_DIM
IN_COLS = A_Q_W + 2 * A_KV_W + 3 * B_W
WINDOW = 128
BLOCK = 128
ROPE_THETA = 500000.0
ROPE_DIM = HEAD_DIM // 4
GRID_W = 64
NA_KH_MAX = 8
NA_KW = 16
N_MEM = 256
CA_HEADS = 4
CA_HEAD_DIM = HEAD_DIM
CA_W = CA_HEADS * CA_HEAD_DIM
D_FF = ((8 * D_MODEL // 3 + 255) // 256) * 256
EPS = 1e-6
NEG = -1e30

kernel_name = "hymba_style_window_neighbourhood_encoder"


def rms_norm(x, g):
    xf = x.astype(jnp.float32)
    y = xf * lax.rsqrt(jnp.mean(xf * xf, axis=-1, keepdims=True) + EPS)
    g32 = g.astype(jnp.float32).reshape((1,) * (x.ndim - 1) + (x.shape[-1],))
    return (y * g32).astype(x.dtype)


def swiglu(h, w_gate, w_up, w_down):
    return (jax.nn.silu(h @ w_gate) * (h @ w_up)) @ w_down


def partial_rope(x, pos):
    half = ROPE_DIM // 2
    inv = 1.0 / (ROPE_THETA ** (jnp.arange(half, dtype=jnp.float32) / half))
    ang = pos.astype(jnp.float32)[:, None] * inv[None, :]
    cos = jnp.cos(ang)[None, :, None, :]
    sin = jnp.sin(ang)[None, :, None, :]
    xr = x[..., :ROPE_DIM].astype(jnp.float32)
    x1, x2 = xr[..., :half], xr[..., half:]
    rot = jnp.concatenate([x1 * cos - x2 * sin, x2 * cos + x1 * sin], axis=-1).astype(x.dtype)
    return jnp.concatenate([rot, x[..., ROPE_DIM:]], axis=-1)


def window_gqa_sink(q, k, v, sink):
    bsz, s_len, hq, d = q.shape
    hkv = k.shape[2]
    g = hq // hkv
    nb = s_len // BLOCK
    qb = q.reshape(bsz, nb, BLOCK, hkv, g, d)

    def band(t):
        tp = jnp.pad(t, ((0, 0), (BLOCK, BLOCK), (0, 0), (0, 0))).reshape(bsz, nb + 2, BLOCK, hkv, d)
        return jnp.concatenate([tp[:, :-2], tp[:, 1:-1], tp[:, 2:]], axis=2)

    kb, vb = band(k), band(v)
    s = jnp.einsum('bnqhgd,bnkhd->bnhgqk', qb, kb, preferred_element_type=jnp.float32) * (d ** -0.5)
    blk = jnp.arange(nb)[:, None, None]
    qpos = blk * BLOCK + jnp.arange(BLOCK)[None, :, None]
    kpos = (blk - 1) * BLOCK + jnp.arange(3 * BLOCK)[None, None, :]
    valid = (jnp.abs(kpos - qpos) <= WINDOW) & (kpos >= 0) & (kpos < s_len)
    s = jnp.where(valid[None, :, None, None], s, NEG)
    sink_l = sink.astype(jnp.float32).reshape(hkv, g)[None, None, :, :, None, None]
    m = jnp.maximum(jnp.max(s, axis=-1, keepdims=True), sink_l)
    p = jnp.exp(s - m)
    denom = jnp.sum(p, axis=-1, keepdims=True) + jnp.exp(sink_l - m)
    p = (p / denom).astype(v.dtype)
    o = jnp.einsum('bnhgqk,bnkhd->bnqhgd', p, vb)
    return o.reshape(bsz, s_len, hq * d)


def neighbourhood_attn(q, k, v, rel_bias):
    bsz, s_len, h, d = q.shape
    rows = s_len // GRID_W
    kh = min(NA_KH_MAX, rows)
    qg = q.reshape(bsz, rows, GRID_W, h, d)
    kg = k.reshape(bsz, rows, GRID_W, h, d)
    vg = v.reshape(bsz, rows, GRID_W, h, d)
    col = jnp.arange(GRID_W)
    col_start = jnp.clip(col - NA_KW // 2, 0, GRID_W - NA_KW)
    col_idx = col_start[:, None] + jnp.arange(NA_KW)[None, :]
    dc = col_idx - col[:, None] + (NA_KW - 1)
    scale = d ** -0.5

    def one_row(r):
        r0 = jnp.clip(r - kh // 2, 0, rows - kh)
        kband = lax.dynamic_slice_in_dim(kg, r0, kh, axis=1)
        vband = lax.dynamic_slice_in_dim(vg, r0, kh, axis=1)
        kwin = kband[:, :, col_idx]
        vwin = vband[:, :, col_idx]
        qr = lax.dynamic_index_in_dim(qg, r, axis=1, keepdims=False)
        s = jnp.einsum('bchd,bjcwhd->bhcjw', qr, kwin, preferred_element_type=jnp.float32) * scale
        dr = r0 + jnp.arange(kh) - r + (NA_KH_MAX - 1)
        bias = rel_bias[:, dr[:, None, None], dc[None, :, :]]
        s = s + jnp.transpose(bias, (0, 2, 1, 3))[None].astype(jnp.float32)
        p = jax.nn.softmax(s.reshape(bsz, h, GRID_W, kh * NA_KW), axis=-1)
        p = p.reshape(s.shape).astype(v.dtype)
        return jnp.einsum('bhcjw,bjcwhd->bchd', p, vwin)

    out = lax.map(one_row, jnp.arange(rows))
    return jnp.transpose(out, (1, 0, 2, 3, 4)).reshape(bsz, s_len, h * d)


def memory_cross_attn(h, mem_n, w_q, w_kv, w_o):
    bsz, s_len, _ = h.shape
    n_mem = mem_n.shape[1]
    q = (h @ w_q).reshape(bsz, s_len, CA_HEADS, CA_HEAD_DIM)
    kv = (mem_n @ w_kv).reshape(bsz, n_mem, 2, CA_HEADS, CA_HEAD_DIM)
    k, v = kv[:, :, 0], kv[:, :, 1]
    s = jnp.einsum('bshd,bmhd->bhsm', q, k, preferred_element_type=jnp.float32) * (CA_HEAD_DIM ** -0.5)
    p = jax.nn.softmax(s, axis=-1).astype(v.dtype)
    o = jnp.einsum('bhsm,bmhd->bshd', p, v).reshape(bsz, s_len, CA_W)
    return o @ w_o


def encoder_trunk(x, mem, p):
    bsz, s_len = x.shape[0], x.shape[1]
    pos = jnp.arange(s_len)
    o0 = A_Q_W
    o1 = o0 + A_KV_W
    o2 = o1 + A_KV_W
    o3 = o2 + B_W
    o4 = o3 + B_W
    for l in range(DEPTH):
        x = x + 0.5 * swiglu(rms_norm(x, p['ffn1_norm'][l]), p['ffn1_w_gate'][l], p['ffn1_w_up'][l], p['ffn1_w_down'][l])
        h = rms_norm(x, p['mix_norm'][l])
        proj = h @ p['w_in'][l]
        qa = proj[..., :o0]
        ka = proj[..., o0:o1]
        va = proj[..., o1:o2]
        qb = proj[..., o2:o3]
        kb = proj[..., o3:o4]
        vb = proj[..., o4:]
        qa = partial_rope(qa.reshape(bsz, s_len, A_Q_HEADS, HEAD_DIM), pos)
        ka = partial_rope(ka.reshape(bsz, s_len, A_KV_HEADS, HEAD_DIM), pos)
        va = va.reshape(bsz, s_len, A_KV_HEADS, HEAD_DIM)
        oa = window_gqa_sink(qa, ka, va, p['a_sink'][l])
        ob = neighbourhood_attn(qb.reshape(bsz, s_len, B_HEADS, HEAD_DIM),
                                kb.reshape(bsz, s_len, B_HEADS, HEAD_DIM),
                                vb.reshape(bsz, s_len, B_HEADS, HEAD_DIM),
                                p['b_rel_bias'][l])
        merged = jnp.concatenate([rms_norm(oa, p['a_out_norm'][l]), rms_norm(ob, p['b_out_norm'][l])], axis=-1)
        x = x + merged @ p['w_out'][l]
        x = x + memory_cross_attn(rms_norm(x, p['ca_norm'][l]), rms_norm(mem, p['mem_norm'][l]),
                                  p['ca_w_q'][l], p['ca_w_kv'][l], p['ca_w_o'][l])
        x = x + 0.5 * swiglu(rms_norm(x, p['ffn2_norm'][l]), p['ffn2_w_gate'][l], p['ffn2_w_up'][l], p['ffn2_w_down'][l])
    return rms_norm(x, p['final_norm'])


def setup_inputs(seed: int = 0) -> dict:
    key = jax.random.key(seed)
    ks = jax.random.split(key, 32)
    f32 = jnp.float32

    def nrm(k, shape, scale):
        return jax.random.normal(k, shape, f32) * scale

    def gain(k, shape):
        return 1.0 + 0.02 * jax.random.normal(k, shape, f32)

    L, D = DEPTH, D_MODEL
    return {
        'x_prompt': nrm(ks[0], (BATCH, SEQ, D), 1.0),
        'x_sample': nrm(ks[1], (DEC_BATCH, DEC_SEQ, D), 1.0),
        'mem_prompt': nrm(ks[2], (BATCH, N_MEM, D), 1.0),
        'mem_sample': nrm(ks[3], (DEC_BATCH, N_MEM, D), 1.0),
        'ffn1_norm': gain(ks[4], (L, D)),
        'ffn1_w_gate': nrm(ks[5], (L, D, D_FF), D ** -0.5),
        'ffn1_w_up': nrm(ks[6], (L, D, D_FF), D ** -0.5),
        'ffn1_w_down': nrm(ks[7], (L, D_FF, D), D_FF ** -0.5),
        'mix_norm': gain(ks[8], (L, D)),
        'w_in': nrm(ks[9], (L, D, IN_COLS), D ** -0.5),
        'a_sink': nrm(ks[10], (L, A_Q_HEADS), 0.5),
        'b_rel_bias': nrm(ks[11], (L, B_HEADS, 2 * NA_KH_MAX - 1, 2 * NA_KW - 1), 0.1),
        'a_out_norm': gain(ks[12], (L, A_Q_W)),
        'b_out_norm': gain(ks[13], (L, B_W)),
        'w_out': nrm(ks[14], (L, D, D), D ** -0.5),
        'ca_norm': gain(ks[15], (L, D)),
        'mem_norm': gain(ks[16], (L, D)),
        'ca_w_q': nrm(ks[17], (L, D, CA_W), D ** -0.5),
        'ca_w_kv': nrm(ks[18], (L, D, 2 * CA_W), D ** -0.5),
        'ca_w_o': nrm(ks[19], (L, CA_W, D), CA_W ** -0.5),
        'ffn2_norm': gain(ks[20], (L, D)),
        'ffn2_w_gate': nrm(ks[21], (L, D, D_FF), D ** -0.5),
        'ffn2_w_up': nrm(ks[22], (L, D, D_FF), D ** -0.5),
        'ffn2_w_down': nrm(ks[23], (L, D_FF, D), D_FF ** -0.5),
        'final_norm': gain(ks[24], (D,)),
    }


def reference(x_prompt, x_sample, mem_prompt, mem_sample, ffn1_norm, ffn1_w_gate, ffn1_w_up, ffn1_w_down,
              mix_norm, w_in, a_sink, b_rel_bias, a_out_norm, b_out_norm, w_out, ca_norm, mem_norm,
              ca_w_q, ca_w_kv, ca_w_o, ffn2_norm, ffn2_w_gate, ffn2_w_up, ffn2_w_down, final_norm):
    params = {
        'ffn1_norm': ffn1_norm, 'ffn1_w_gate': ffn1_w_gate, 'ffn1_w_up': ffn1_w_up, 'ffn1_w_down': ffn1_w_down,
        'mix_norm': mix_norm, 'w_in': w_in, 'a_sink': a_sink, 'b_rel_bias': b_rel_bias,
        'a_out_norm': a_out_norm, 'b_out_norm': b_out_norm, 'w_out': w_out,
        'ca_norm': ca_norm, 'mem_norm': mem_norm, 'ca_w_q': ca_w_q, 'ca_w_kv': ca_w_kv, 'ca_w_o': ca_w_o,
        'ffn2_norm': ffn2_norm, 'ffn2_w_gate': ffn2_w_gate, 'ffn2_w_up': ffn2_w_up, 'ffn2_w_down': ffn2_w_down,
        'final_norm': final_norm,
    }
    y_prompt = encoder_trunk(x_prompt, mem_prompt, params)
    y_sample = encoder_trunk(x_sample, mem_sample, params)
    return (y_prompt, y_sample)
```

```python
import functools

import jax
import jax.numpy as jnp
from jax import lax
from jax.experimental import pallas as pl
from jax.experimental.pallas import tpu as pltpu

F32 = jnp.float32
BF16 = jnp.bfloat16

HEAD_DIM = 128
WINDOW = 128
ROPE_THETA = 500000.0
ROPE_DIM = HEAD_DIM // 4
GRID_W = 64
NA_KH = 8
NA_KW = 16
CA_HEADS = 4
EPS = 1e-6
NEG = -1e30
SCALE = HEAD_DIM ** -0.5

V7X_VMEM_BYTES = 64 * 1024 * 1024
NORM_ROWS = 16


def _vmem_limit(nbytes):
    return int(min(nbytes + 12 * 1024 * 1024, V7X_VMEM_BYTES - 4 * 1024 * 1024))


def _rms(x, gain):
    ms = jnp.mean(x * x, axis=-1, keepdims=True)
    return x * lax.rsqrt(ms + EPS) * gain


def _norm_rows(src_ref, gain_ref, dst_ref, rows):
    gain = gain_ref[...]

    def body(i, carry):
        r = pl.multiple_of(i * NORM_ROWS, NORM_ROWS)
        x = src_ref[pl.ds(r, NORM_ROWS), :]
        dst_ref[pl.ds(r, NORM_ROWS), :] = _rms(x, gain).astype(dst_ref.dtype)
        return carry

    lax.fori_loop(0, rows // NORM_ROWS, body, 0)


def _dot(a, b):
    return jnp.dot(a, b, preferred_element_type=F32)


def _dot_nt(a, b):
    return lax.dot_general(a, b, (((1,), (1,)), ((), ())), preferred_element_type=F32)


def _ffn_kernel(x_ref, gin_ref, wg_ref, wu_ref, wd_ref, gfin_ref, o_ref, h_ref, *, tm, nf, ncol, final):
    f = pl.program_id(1)

    @pl.when(f == 0)
    def _():
        _norm_rows(x_ref, gin_ref, h_ref, tm)

    h = h_ref[...]
    g = _dot(h, wg_ref[...])
    u = _dot(h, wu_ref[...])
    act = (g / (1.0 + jnp.exp(-g)) * u).astype(BF16)
    d = o_ref.shape[1]

    @pl.when(f == 0)
    def _():
        for c in range(0, d, ncol):
            o_ref[:, c:c + ncol] = _dot(act, wd_ref[:, c:c + ncol])

    @pl.when(f > 0)
    def _():
        for c in range(0, d, ncol):
            o_ref[:, c:c + ncol] += _dot(act, wd_ref[:, c:c + ncol])

    @pl.when(f == nf - 1)
    def _():
        gfin = gfin_ref[...]

        def body(i, carry):
            r = pl.multiple_of(i * NORM_ROWS, NORM_ROWS)
            y = x_ref[pl.ds(r, NORM_ROWS), :] + 0.5 * o_ref[pl.ds(r, NORM_ROWS), :]
            if final:
                y = _rms(y, gfin)
            o_ref[pl.ds(r, NORM_ROWS), :] = y
            return carry

        lax.fori_loop(0, tm // NORM_ROWS, body, 0)


def _ffn(x, gin, wg, wu, wd, gfin, *, final, tm=512, tf=256, ncol=512):
    m, d = x.shape
    dff = wg.shape[1]
    tm = min(tm, m)
    nf = dff // tf
    assert m % tm == 0 and dff % tf == 0 and d % ncol == 0
    vmem = 2 * tm * d * 4 + 2 * tm * d * 4 + tm * d * 2 + 2 * 3 * d * tf * 2
    return pl.pallas_call(
        functools.partial(_ffn_kernel, tm=tm, nf=nf, ncol=ncol, final=final),
        grid=(m // tm, nf),
        in_specs=[
            pl.BlockSpec((tm, d), lambda i, f: (i, 0)),
            pl.BlockSpec((1, d), lambda i, f: (0, 0)),
            pl.BlockSpec((d, tf), lambda i, f: (0, f)),
            pl.BlockSpec((d, tf), lambda i, f: (0, f)),
            pl.BlockSpec((tf, d), lambda i, f: (f, 0)),
            pl.BlockSpec((1, d), lambda i, f: (0, 0)),
        ],
        out_specs=pl.BlockSpec((tm, d), lambda i, f: (i, 0)),
        out_shape=jax.ShapeDtypeStruct((m, d), F32),
        scratch_shapes=[pltpu.VMEM((tm, d), BF16)],
        compiler_params=pltpu.CompilerParams(
            dimension_semantics=("parallel", "arbitrary"), vmem_limit_bytes=_vmem_limit(vmem)),
        name="ffn_final" if final else "ffn",
    )(x, gin, wg, wu, wd, gfin)


def _rope_tables(s_len):
    half = ROPE_DIM // 2
    inv = 1.0 / (ROPE_THETA ** (jnp.arange(half, dtype=F32) / half))
    ang = jnp.arange(s_len).astype(F32)[:, None] * inv[None, :]
    cos, sin = jnp.cos(ang), jnp.sin(ang)
    rest = HEAD_DIM - ROPE_DIM
    cos_t = jnp.concatenate([cos, cos, jnp.ones((s_len, rest), F32)], axis=1)
    sin_t = jnp.concatenate([-sin, sin, jnp.zeros((s_len, rest), F32)], axis=1)
    return cos_t, sin_t


def _proj_kernel(x_ref, gin_ref, w_ref, cos_ref, sin_ref, o_ref, h_ref, *, tm, tn, rope_ranges):
    j = pl.program_id(1)

    @pl.when(j == 0)
    def _():
        _norm_rows(x_ref, gin_ref, h_ref, tm)

    acc = _dot(h_ref[...], w_ref[...])
    col = j * tn
    is_rope = functools.reduce(jnp.logical_or, [(col >= lo) & (col < hi) for lo, hi in rope_ranges])

    @pl.when(is_rope)
    def _():
        cos = cos_ref[...]
        sin = sin_ref[...]
        lane = lax.broadcasted_iota(jnp.int32, (tm, HEAD_DIM), 1)
        half = ROPE_DIM // 2
        for c in range(0, tn, HEAD_DIM):
            xh = acc[:, c:c + HEAD_DIM]
            partner = jnp.where(lane < half, pltpu.roll(xh, HEAD_DIM - half, 1), pltpu.roll(xh, half, 1))
            rot = xh * cos + partner * sin
            o_ref[:, c:c + HEAD_DIM] = jnp.where(lane < ROPE_DIM, rot, xh).astype(o_ref.dtype)

    @pl.when(jnp.logical_not(is_rope))
    def _():
        o_ref[...] = acc.astype(o_ref.dtype)


def _proj_in(x, gin, w, cos_t, sin_t, *, s_len, rope_ranges, tm=512, tn=512):
    m, d = x.shape
    n = w.shape[1]
    tm = min(tm, s_len)
    assert m % tm == 0 and s_len % tm == 0 and n % tn == 0
    assert all(lo % tn == 0 and hi % tn == 0 for lo, hi in rope_ranges)
    per_seq = s_len // tm
    vmem = 2 * tm * d * 4 + tm * d * 2 + 2 * d * tn * 2 + 2 * tm * tn * 2 + 4 * tm * HEAD_DIM * 4
    return pl.pallas_call(
        functools.partial(_proj_kernel, tm=tm, tn=tn, rope_ranges=rope_ranges),
        grid=(m // tm, n // tn),
        in_specs=[
            pl.BlockSpec((tm, d), lambda i, j: (i, 0)),
            pl.BlockSpec((1, d), lambda i, j: (0, 0)),
            pl.BlockSpec((d, tn), lambda i, j: (0, j)),
            pl.BlockSpec((tm, HEAD_DIM), lambda i, j: (i % per_seq, 0)),
            pl.BlockSpec((tm, HEAD_DIM), lambda i, j: (i % per_seq, 0)),
        ],
        out_specs=pl.BlockSpec((tm, tn), lambda i, j: (i, j)),
        out_shape=jax.ShapeDtypeStruct((m, n), BF16),
        scratch_shapes=[pltpu.VMEM((tm, d), BF16)],
        compiler_params=pltpu.CompilerParams(
            dimension_semantics=("parallel", "arbitrary"), vmem_limit_bytes=_vmem_limit(vmem)),
        name="proj_in",
    )(x, gin, w, cos_t, sin_t)


def _attn_a_kernel(sink_ref, q_ref, kp_ref, kc_ref, kn_ref, vp_ref, vc_ref, vn_ref, gain_ref, o_ref, oacc_ref,
                   *, nb, n_kv, group):
    n = pl.program_id(1)
    blk = WINDOW
    rows = group * blk
    rr = lax.broadcasted_iota(jnp.int32, (rows, 3 * blk), 0) & (blk - 1)
    cc = lax.broadcasted_iota(jnp.int32, (rows, 3 * blk), 1)
    lo = jnp.where(n == 0, blk, 0)
    hi = jnp.where(n == nb - 1, 2 * blk, 3 * blk)
    valid = (cc >= rr) & (cc <= rr + 2 * WINDOW) & (cc >= lo) & (cc < hi)
    grp = lax.broadcasted_iota(jnp.int32, (rows, 1), 0) // blk
    for kv in range(n_kv):
        ks = slice(kv * HEAD_DIM, (kv + 1) * HEAD_DIM)
        k = jnp.concatenate([kp_ref[:, ks], kc_ref[:, ks], kn_ref[:, ks]], axis=0)
        v = jnp.concatenate([vp_ref[:, ks], vc_ref[:, ks], vn_ref[:, ks]], axis=0)
        q = jnp.concatenate(
            [q_ref[:, (kv * group + g) * HEAD_DIM:(kv * group + g + 1) * HEAD_DIM] for g in range(group)], axis=0)
        s = _dot_nt(q, k) * SCALE
        s = jnp.where(valid, s, NEG)
        sink = jnp.zeros((rows, 1), F32)
        for g in range(group):
            sink = jnp.where(grp == g, sink_ref[kv * group + g], sink)
        mx = jnp.maximum(jnp.max(s, axis=-1, keepdims=True), sink)
        p = jnp.exp(s - mx)
        den = jnp.sum(p, axis=-1, keepdims=True) + jnp.exp(sink - mx)
        o = _dot(p.astype(BF16), v) / den
        for g in range(group):
            h0 = (kv * group + g) * HEAD_DIM
            oacc_ref[:, h0:h0 + HEAD_DIM] = o[g * blk:(g + 1) * blk, :]
    o_ref[...] = _rms(oacc_ref[...], gain_ref[...]).astype(o_ref.dtype)


def _attn_a(proj, sink, gain, *, bsz, s_len, q_col, k_col, v_col, n_q, n_kv):
    m = proj.shape[0]
    blk = WINDOW
    nb = s_len // blk
    qw, kw = n_q * HEAD_DIM, n_kv * HEAD_DIM
    assert s_len % blk == 0 and q_col % qw == 0 and k_col % kw == 0 and v_col % kw == 0
    qb, kb, vb = q_col // qw, k_col // kw, v_col // kw

    def prev_map(col):
        return lambda b, n: (b * nb + jnp.maximum(n - 1, 0), col)

    def cur_map(col):
        return lambda b, n: (b * nb + n, col)

    def next_map(col):
        return lambda b, n: (b * nb + jnp.minimum(n + 1, nb - 1), col)

    return pl.pallas_call(
        functools.partial(_attn_a_kernel, nb=nb, n_kv=n_kv, group=n_q // n_kv),
        grid=(bsz, nb),
        in_specs=[
            pl.BlockSpec(memory_space=pltpu.SMEM),
            pl.BlockSpec((blk, qw), cur_map(qb)),
            pl.BlockSpec((blk, kw), prev_map(kb)),
            pl.BlockSpec((blk, kw), cur_map(kb)),
            pl.BlockSpec((blk, kw), next_map(kb)),
            pl.BlockSpec((blk, kw), prev_map(vb)),
            pl.BlockSpec((blk, kw), cur_map(vb)),
            pl.BlockSpec((blk, kw), next_map(vb)),
            pl.BlockSpec((1, qw), lambda b, n: (0, 0)),
        ],
        out_specs=pl.BlockSpec((blk, qw), lambda b, n: (b * nb + n, 0)),
        out_shape=jax.ShapeDtypeStruct((m, qw), BF16),
        scratch_shapes=[pltpu.VMEM((blk, qw), F32)],
        compiler_params=pltpu.CompilerParams(dimension_semantics=("parallel", "arbitrary")),
        name="attn_a",
    )(sink, proj, proj, proj, proj, proj, proj, proj, gain)


def _bias_tables(rel_bias):
    n_h, n_dr, _ = rel_bias.shape
    col = jnp.arange(GRID_W)
    col_start = jnp.clip(col - NA_KW // 2, 0, GRID_W - NA_KW)
    inside = (col[None, :] >= col_start[:, None]) & (col[None, :] < col_start[:, None] + NA_KW)
    dc = jnp.clip(col[None, :] - col[:, None] + (NA_KW - 1), 0, 2 * NA_KW - 2)
    tc = jnp.where(inside[None, None], rel_bias.astype(F32)[:, :, dc], NEG)
    pairs = jnp.concatenate([tc[:, :-1], tc[:, 1:]], axis=-1)
    return pairs.reshape(n_h * (n_dr - 1), GRID_W, 2 * GRID_W)


def _attn_b_kernel(q_ref, k_ref, v_ref, bias_ref, gain_ref, o_ref, orow_ref, *, rows, n_h, rpb):
    rb = pl.program_id(1)
    band0 = jnp.clip(rpb * rb - NA_KH // 2, 0, rows - 2 * rpb)
    n_pair = 2 * NA_KH - 2
    gain = gain_ref[...]

    def row_body(ri, carry):
        r = rpb * rb + ri
        r0 = jnp.clip(r - NA_KH // 2, 0, rows - NA_KH)
        off = pl.multiple_of((r0 - band0) * GRID_W, GRID_W)
        qoff = pl.multiple_of(ri * GRID_W, GRID_W)
        dr0 = r0 - r + (NA_KH - 1)
        for h in range(n_h):
            hs = slice(h * HEAD_DIM, (h + 1) * HEAD_DIM)
            q = q_ref[pl.ds(qoff, GRID_W), hs]
            kk = k_ref[pl.ds(off, NA_KH * GRID_W), hs]
            vv = v_ref[pl.ds(off, NA_KH * GRID_W), hs]
            bias = jnp.concatenate([bias_ref[h * n_pair + dr0 + 2 * t] for t in range(NA_KH // 2)], axis=1)
            s = _dot_nt(q, kk) * SCALE + bias
            mx = jnp.max(s, axis=-1, keepdims=True)
            p = jnp.exp(s - mx)
            den = jnp.sum(p, axis=-1, keepdims=True)
            orow_ref[:, hs] = _dot(p.astype(BF16), vv) / den
        o_ref[pl.ds(qoff, GRID_W), :] = _rms(orow_ref[...], gain).astype(o_ref.dtype)
        return carry

    lax.fori_loop(0, rpb, row_body, 0)


def _attn_b(proj, bias_t, gain, *, bsz, s_len, q_col, k_col, v_col, n_h, rpb=8):
    m = proj.shape[0]
    rows = s_len // GRID_W
    width = n_h * HEAD_DIM
    assert rows % rpb == 0 and rows >= 2 * rpb and rpb >= NA_KH and q_col % width == 0
    nrb = rows // rpb
    band = 2 * rpb * GRID_W
    tq = rpb * GRID_W

    def band_map(col):
        def index(b, rb):
            start = jnp.clip(rpb * rb - NA_KH // 2, 0, rows - 2 * rpb)
            return (pl.multiple_of((b * rows + start) * GRID_W, GRID_W), col)
        return index

    vmem = 2 * (tq * width * 2 + 2 * band * width * 2 + tq * width * 2) + 2 * bias_t.size * 4 + GRID_W * width * 4
    return pl.pallas_call(
        functools.partial(_attn_b_kernel, rows=rows, n_h=n_h, rpb=rpb),
        grid=(bsz, nrb),
        in_specs=[
            pl.BlockSpec((tq, width), lambda b, rb: (b * nrb + rb, q_col // width)),
            pl.BlockSpec((pl.Element(band), pl.Element(width)), band_map(k_col)),
            pl.BlockSpec((pl.Element(band), pl.Element(width)), band_map(v_col)),
            pl.BlockSpec(bias_t.shape, lambda b, rb: (0, 0, 0)),
            pl.BlockSpec((1, width), lambda b, rb: (0, 0)),
        ],
        out_specs=pl.BlockSpec((tq, width), lambda b, rb: (b * nrb + rb, 0)),
        out_shape=jax.ShapeDtypeStruct((m, width), BF16),
        scratch_shapes=[pltpu.VMEM((GRID_W, width), F32)],
        compiler_params=pltpu.CompilerParams(
            dimension_semantics=("parallel", "arbitrary"), vmem_limit_bytes=_vmem_limit(vmem)),
        name="attn_b",
    )(proj, proj, proj, bias_t, gain)


def _out_proj_kernel(a_ref, b_ref, wa_ref, wb_ref, x_ref, o_ref):
    o_ref[...] = x_ref[...] + (_dot(a_ref[...], wa_ref[...]) + _dot(b_ref[...], wb_ref[...]))


def _out_proj(oa, ob, w, x, *, tm=1024, tn=512):
    m, d = x.shape
    ka, kb = oa.shape[1], ob.shape[1]
    tm = min(tm, m)
    assert ka == kb and m % tm == 0 and d % tn == 0
    vmem = 2 * (2 * tm * ka * 2 + 2 * ka * tn * 2 + 2 * tm * tn * 4)
    return pl.pallas_call(
        _out_proj_kernel,
        grid=(m // tm, d // tn),
        in_specs=[
            pl.BlockSpec((tm, ka), lambda i, j: (i, 0)),
            pl.BlockSpec((tm, kb), lambda i, j: (i, 0)),
            pl.BlockSpec((ka, tn), lambda i, j: (0, j)),
            pl.BlockSpec((kb, tn), lambda i, j: (1, j)),
            pl.BlockSpec((tm, tn), lambda i, j: (i, j)),
        ],
        out_specs=pl.BlockSpec((tm, tn), lambda i, j: (i, j)),
        out_shape=jax.ShapeDtypeStruct((m, d), F32),
        compiler_params=pltpu.CompilerParams(
            dimension_semantics=("parallel", "arbitrary"), vmem_limit_bytes=_vmem_limit(vmem)),
        name="out_proj",
    )(oa, ob, w, w, x)


def _mem_kv_kernel(mem_ref, gain_ref, w_ref, o_ref, h_ref, *, tm):
    _norm_rows(mem_ref, gain_ref, h_ref, tm)
    o_ref[...] = _dot(h_ref[...], w_ref[...]).astype(o_ref.dtype)


def _mem_kv(mem, gain, w, *, tm=256):
    m, d = mem.shape
    n = w.shape[1]
    tm = min(tm, m)
    assert m % tm == 0
    vmem = 2 * tm * d * 4 + tm * d * 2 + 2 * d * n * 2 + 2 * tm * n * 2
    return pl.pallas_call(
        functools.partial(_mem_kv_kernel, tm=tm),
        grid=(m // tm,),
        in_specs=[
            pl.BlockSpec((tm, d), lambda i: (i, 0)),
            pl.BlockSpec((1, d), lambda i: (0, 0)),
            pl.BlockSpec((d, n), lambda i: (0, 0)),
        ],
        out_specs=pl.BlockSpec((tm, n), lambda i: (i, 0)),
        out_shape=jax.ShapeDtypeStruct((m, n), BF16),
        scratch_shapes=[pltpu.VMEM((tm, d), BF16)],
        compiler_params=pltpu.CompilerParams(
            dimension_semantics=("parallel",), vmem_limit_bytes=_vmem_limit(vmem)),
        name="mem_kv",
    )(mem, gain, w)


def _cross_kernel(x_ref, gain_ref, wq_ref, k_ref, v_ref, wo_ref, o_ref, h_ref, att_ref, *, tm):
    _norm_rows(x_ref, gain_ref, h_ref, tm)
    q = _dot(h_ref[...], wq_ref[...]).astype(BF16)
    for h in range(CA_HEADS):
        hs = slice(h * HEAD_DIM, (h + 1) * HEAD_DIM)
        s = _dot_nt(q[:, hs], k_ref[:, hs]) * SCALE
        mx = jnp.max(s, axis=-1, keepdims=True)
        p = jnp.exp(s - mx)
        den = jnp.sum(p, axis=-1, keepdims=True)
        att_ref[:, hs] = (_dot(p.astype(BF16), v_ref[:, hs]) / den).astype(BF16)
    o_ref[...] = x_ref[...] + _dot(att_ref[...], wo_ref[...])


def _cross(x, gain, wq, kv, wo, *, s_len, n_mem, tm=256):
    m, d = x.shape
    caw = wq.shape[1]
    tm = min(tm, s_len)
    assert m % tm == 0 and s_len % tm == 0 and caw == CA_HEADS * HEAD_DIM
    per_seq = s_len // tm
    vmem = 4 * tm * d * 4 + tm * d * 2 + 2 * 2 * d * caw * 2 + 4 * n_mem * caw * 2 + tm * caw * 2
    return pl.pallas_call(
        functools.partial(_cross_kernel, tm=tm),
        grid=(m // tm,),
        in_specs=[
            pl.BlockSpec((tm, d), lambda i: (i, 0)),
            pl.BlockSpec((1, d), lambda i: (0, 0)),
            pl.BlockSpec((d, caw), lambda i: (0, 0)),
            pl.BlockSpec((n_mem, caw), lambda i: (i // per_seq, 0)),
            pl.BlockSpec((n_mem, caw), lambda i: (i // per_seq, 1)),
            pl.BlockSpec((caw, d), lambda i: (0, 0)),
        ],
        out_specs=pl.BlockSpec((tm, d), lambda i: (i, 0)),
        out_shape=jax.ShapeDtypeStruct((m, d), F32),
        scratch_shapes=[pltpu.VMEM((tm, d), BF16), pltpu.VMEM((tm, caw), BF16)],
        compiler_params=pltpu.CompilerParams(
            dimension_semantics=("parallel",), vmem_limit_bytes=_vmem_limit(vmem)),
        name="cross",
    )(x, gain, wq, kv, kv, wo)


def _trunk(x3, mem3, p):
    bsz, s_len, d = x3.shape
    n_mem = mem3.shape[1]
    x = x3.reshape(bsz * s_len, d)
    mem = mem3.reshape(bsz * n_mem, d)
    cos_t, sin_t = _rope_tables(s_len)

    x = _ffn(x, p["ffn1_norm"], p["ffn1_w_gate"], p["ffn1_w_up"], p["ffn1_w_down"], p["final_norm"], final=False)
    proj = _proj_in(x, p["mix_norm"], p["w_in"], cos_t, sin_t, s_len=s_len, rope_ranges=p["rope_ranges"])
    c = p["cols"]
    oa = _attn_a(proj, p["a_sink"], p["a_out_norm"], bsz=bsz, s_len=s_len,
                 q_col=c["qa"], k_col=c["ka"], v_col=c["va"], n_q=p["a_q_heads"], n_kv=p["a_kv_heads"])
    ob = _attn_b(proj, p["b_bias"], p["b_out_norm"], bsz=bsz, s_len=s_len,
                 q_col=c["qb"], k_col=c["kb"], v_col=c["vb"], n_h=p["b_heads"])
    x = _out_proj(oa, ob, p["w_out"], x)
    kv = _mem_kv(mem, p["mem_norm"], p["ca_w_kv"])
    x = _cross(x, p["ca_norm"], p["ca_w_q"], kv, p["ca_w_o"], s_len=s_len, n_mem=n_mem)
    y = _ffn(x, p["ffn2_norm"], p["ffn2_w_gate"], p["ffn2_w_up"], p["ffn2_w_down"], p["final_norm"], final=True)
    return y.reshape(bsz, s_len, d)


def _prepare(ffn1_norm, ffn1_w_gate, ffn1_w_up, ffn1_w_down, mix_norm, w_in, a_sink, b_rel_bias, a_out_norm,
             b_out_norm, w_out, ca_norm, mem_norm, ca_w_q, ca_w_kv, ca_w_o, ffn2_norm, ffn2_w_gate, ffn2_w_up,
             ffn2_w_down, final_norm):
    d = w_in.shape[1]
    n_heads = d // HEAD_DIM
    a_q = n_heads // 2
    a_kv = a_q // 4
    b_h = n_heads - a_q
    aq_w, akv_w, b_w = a_q * HEAD_DIM, a_kv * HEAD_DIM, b_h * HEAD_DIM
    o0, o1, o2 = aq_w, aq_w + akv_w, aq_w + 2 * akv_w
    w = w_in[0]
    w_perm = jnp.concatenate([w[:, :o0], w[:, o2:], w[:, o0:o2]], axis=1).astype(BF16)
    cols = {"qa": 0, "qb": aq_w, "kb": aq_w + b_w, "vb": aq_w + 2 * b_w, "ka": aq_w + 3 * b_w,
            "va": aq_w + 3 * b_w + akv_w}

    def gain(g):
        return g.reshape(1, -1).astype(F32)

    def wt(a):
        return a[0].astype(BF16)

    return {
        "ffn1_norm": gain(ffn1_norm[0]), "ffn1_w_gate": wt(ffn1_w_gate), "ffn1_w_up": wt(ffn1_w_up),
        "ffn1_w_down": wt(ffn1_w_down),
        "mix_norm": gain(mix_norm[0]), "w_in": w_perm, "cols": cols,
        "rope_ranges": ((cols["qa"], cols["qa"] + aq_w), (cols["ka"], cols["ka"] + akv_w)),
        "a_sink": a_sink[0].astype(F32), "b_bias": _bias_tables(b_rel_bias[0]),
        "a_out_norm": gain(a_out_norm[0]), "b_out_norm": gain(b_out_norm[0]), "w_out": wt(w_out),
        "a_q_heads": a_q, "a_kv_heads": a_kv, "b_heads": b_h,
        "ca_norm": gain(ca_norm[0]), "mem_norm": gain(mem_norm[0]), "ca_w_q": wt(ca_w_q), "ca_w_kv": wt(ca_w_kv),
        "ca_w_o": wt(ca_w_o),
        "ffn2_norm": gain(ffn2_norm[0]), "ffn2_w_gate": wt(ffn2_w_gate), "ffn2_w_up": wt(ffn2_w_up),
        "ffn2_w_down": wt(ffn2_w_down), "final_norm": gain(final_norm),
    }


def kernel(x_prompt, x_sample, mem_prompt, mem_sample, ffn1_norm, ffn1_w_gate, ffn1_w_up, ffn1_w_down, mix_norm, w_in, a_sink, b_rel_bias, a_out_norm, b_out_norm, w_out, ca_norm, mem_norm, ca_w_q, ca_w_kv, ca_w_o, ffn2_norm, ffn2_w_gate, ffn2_w_up, ffn2_w_down, final_norm):
    assert ffn1_w_gate.shape[0] == 1, "single-layer trunk"
    p = _prepare(ffn1_norm, ffn1_w_gate, ffn1_w_up, ffn1_w_down, mix_norm, w_in, a_sink, b_rel_bias, a_out_norm,
                 b_out_norm, w_out, ca_norm, mem_norm, ca_w_q, ca_w_kv, ca_w_o, ffn2_norm, ffn2_w_gate, ffn2_w_up,
                 ffn2_w_down, final_norm)
    return (_trunk(x_prompt, mem_prompt, p), _trunk(x_sample, mem_sample, p))
```

```python
import functools

import jax
import jax.numpy as jnp
from jax import lax
from jax.experimental import pallas as pl
from jax.experimental.pallas import tpu as pltpu

F32 = jnp.float32
BF16 = jnp.bfloat16

HEAD_DIM = 128
WINDOW = 128
ROPE_THETA = 500000.0
ROPE_DIM = HEAD_DIM // 4
GRID_W = 64
NA_KH = 8
NA_KW = 16
CA_HEADS = 4
EPS = 1e-6
NEG = -1e30
SCALE = HEAD_DIM ** -0.5

V7X_VMEM_BYTES = 64 * 1024 * 1024
V7X_LANES = 128
NORM_ROWS = 16
NORM_UNROLL = 4


def _vmem_limit(nbytes):
    return int(min(nbytes + 12 * 1024 * 1024, V7X_VMEM_BYTES - 4 * 1024 * 1024))


def _rms(x, gain):
    ms = jnp.mean(x * x, axis=-1, keepdims=True)
    return x * lax.rsqrt(ms + EPS) * gain


def _norm_rows(src_ref, gain_ref, dst_ref, rows):
    gain = gain_ref[...]

    def body(i, carry):
        r = pl.multiple_of(i * NORM_ROWS, NORM_ROWS)
        x = src_ref[pl.ds(r, NORM_ROWS), :]
        dst_ref[pl.ds(r, NORM_ROWS), :] = _rms(x, gain).astype(dst_ref.dtype)
        return carry

    lax.fori_loop(0, rows // NORM_ROWS, body, 0, unroll=NORM_UNROLL)


def _dot(a, b):
    return jnp.dot(a, b, preferred_element_type=F32)


def _dot_nt(a, b):
    return lax.dot_general(a, b, (((1,), (1,)), ((), ())), preferred_element_type=F32)


def _ffn_kernel(x_ref, gin_ref, *refs, tm, nf, cps, ncol, final):
    w_refs, (gfin_ref, o_ref, h_ref, rs_ref) = refs[:3 * cps], refs[3 * cps:]
    s = pl.program_id(1)
    n_steps = pl.num_programs(1)
    d = o_ref.shape[1]

    @pl.when(s == 0)
    def _():
        _norm_rows(x_ref, gin_ref, h_ref, tm)
        o_ref[...] = jnp.zeros_like(o_ref)

    def chunk(wg_ref, wu_ref, wd_ref):
        h = h_ref[...]
        g = _dot(h, wg_ref[...])
        u = _dot(h, wu_ref[...])
        act = (g / (1.0 + jnp.exp(-g)) * u).astype(BF16)
        for c in range(0, d, ncol):
            o_ref[:, c:c + ncol] += _dot(act, wd_ref[:, c:c + ncol])

    for c in range(cps):
        if (pl.cdiv(nf, cps) - 1) * cps + c < nf:
            chunk(*w_refs[3 * c:3 * c + 3])
        else:
            pl.when(s * cps + c < nf)(functools.partial(chunk, *w_refs[3 * c:3 * c + 3]))

    @pl.when(s == n_steps - 1)
    def _():
        def residual(i, carry):
            rows = pl.ds(pl.multiple_of(i * NORM_ROWS, NORM_ROWS), NORM_ROWS)
            y = x_ref[rows, :] + 0.5 * o_ref[rows, :]
            o_ref[rows, :] = y
            if final:
                ms = jnp.mean(y * y, axis=-1, keepdims=True)
                rs_ref[rows, :] = jnp.broadcast_to(lax.rsqrt(ms + EPS), (NORM_ROWS, rs_ref.shape[1]))
            return carry

        lax.fori_loop(0, tm // NORM_ROWS, residual, 0, unroll=NORM_UNROLL)

        if final:
            gfin = gfin_ref[...]

            def scale(i, carry):
                rows = pl.ds(pl.multiple_of(i * NORM_ROWS, NORM_ROWS), NORM_ROWS)
                o_ref[rows, :] = o_ref[rows, :] * rs_ref[rows, :][:, :1] * gfin
                return carry

            lax.fori_loop(0, tm // NORM_ROWS, scale, 0, unroll=NORM_UNROLL)


def _ffn(x, gin, wg, wu, wd, gfin, *, final, tm=512, tf=256, cps=2, ncol=512):
    m, d = x.shape
    dff = wg.shape[1]
    tm = min(tm, m)
    nf = dff // tf
    cps = min(cps, nf)
    n_steps = pl.cdiv(nf, cps)
    assert m % tm == 0 and dff % tf == 0 and d % ncol == 0

    def chunk_specs(c):
        def col(i, s):
            return (0, jnp.minimum(s * cps + c, nf - 1))

        def row(i, s):
            return (jnp.minimum(s * cps + c, nf - 1), 0)

        return [pl.BlockSpec((d, tf), col), pl.BlockSpec((d, tf), col), pl.BlockSpec((tf, d), row)]

    w_specs = [spec for c in range(cps) for spec in chunk_specs(c)]
    vmem = tm * d * 4 + 2 * tm * d * 4 + tm * d * 2 + 2 * cps * 3 * d * tf * 2
    return pl.pallas_call(
        functools.partial(_ffn_kernel, tm=tm, nf=nf, cps=cps, ncol=ncol, final=final),
        grid=(m // tm, n_steps),
        in_specs=[
            pl.BlockSpec((tm, d), lambda i, s: (i, 0), pipeline_mode=pl.Buffered(1)),
            pl.BlockSpec((1, d), lambda i, s: (0, 0)),
            *w_specs,
            pl.BlockSpec((1, d), lambda i, s: (0, 0)),
        ],
        out_specs=pl.BlockSpec((tm, d), lambda i, s: (i, 0)),
        out_shape=jax.ShapeDtypeStruct((m, d), F32),
        scratch_shapes=[pltpu.VMEM((tm, d), BF16), pltpu.VMEM((tm, V7X_LANES), F32)],
        compiler_params=pltpu.CompilerParams(
            dimension_semantics=("parallel", "arbitrary"), vmem_limit_bytes=_vmem_limit(vmem)),
        name="ffn_final" if final else "ffn",
    )(x, gin, *([wg, wu, wd] * cps), gfin)


def _rope_tables(s_len):
    half = ROPE_DIM // 2
    inv = 1.0 / (ROPE_THETA ** (jnp.arange(half, dtype=F32) / half))
    ang = jnp.arange(s_len).astype(F32)[:, None] * inv[None, :]
    cos, sin = jnp.cos(ang), jnp.sin(ang)
    rest = HEAD_DIM - ROPE_DIM
    cos_t = jnp.concatenate([cos, cos, jnp.ones((s_len, rest), F32)], axis=1)
    sin_t = jnp.concatenate([-sin, sin, jnp.zeros((s_len, rest), F32)], axis=1)
    return cos_t, sin_t


def _proj_kernel(x_ref, gin_ref, w_ref, cos_ref, sin_ref, o_ref, h_ref, *, tm, tn, rope_ranges):
    j = pl.program_id(1)

    @pl.when(j == 0)
    def _():
        _norm_rows(x_ref, gin_ref, h_ref, tm)

    col = j * tn
    is_rope = functools.reduce(jnp.logical_or, [(col >= lo) & (col < hi) for lo, hi in rope_ranges])

    @pl.when(is_rope)
    def _():
        acc = _dot(h_ref[...], w_ref[...])
        cos = cos_ref[...]
        sin = sin_ref[...]
        lane = lax.broadcasted_iota(jnp.int32, (tm, HEAD_DIM), 1)
        half = ROPE_DIM // 2
        for c in range(0, tn, HEAD_DIM):
            xh = acc[:, c:c + HEAD_DIM]
            partner = jnp.where(lane < half, pltpu.roll(xh, HEAD_DIM - half, 1), pltpu.roll(xh, half, 1))
            rot = xh * cos + partner * sin
            o_ref[:, c:c + HEAD_DIM] = jnp.where(lane < ROPE_DIM, rot, xh).astype(o_ref.dtype)

    @pl.when(jnp.logical_not(is_rope))
    def _():
        o_ref[...] = _dot(h_ref[...], w_ref[...]).astype(o_ref.dtype)


def _proj_in(x, gin, w, cos_t, sin_t, *, s_len, rope_ranges, tm=1024, tn=512):
    m, d = x.shape
    n = w.shape[1]
    tm = min(tm, s_len)
    assert m % tm == 0 and s_len % tm == 0 and n % tn == 0
    assert all(lo % tn == 0 and hi % tn == 0 for lo, hi in rope_ranges)
    per_seq = s_len // tm
    vmem = 2 * tm * d * 4 + tm * d * 2 + 2 * d * tn * 2 + 2 * tm * tn * 2 + 4 * tm * HEAD_DIM * 4
    return pl.pallas_call(
        functools.partial(_proj_kernel, tm=tm, tn=tn, rope_ranges=rope_ranges),
        grid=(m // tm, n // tn),
        in_specs=[
            pl.BlockSpec((tm, d), lambda i, j: (i, 0)),
            pl.BlockSpec((1, d), lambda i, j: (0, 0)),
            pl.BlockSpec((d, tn), lambda i, j: (0, j)),
            pl.BlockSpec((tm, HEAD_DIM), lambda i, j: (i % per_seq, 0)),
            pl.BlockSpec((tm, HEAD_DIM), lambda i, j: (i % per_seq, 0)),
        ],
        out_specs=pl.BlockSpec((tm, tn), lambda i, j: (i, j)),
        out_shape=jax.ShapeDtypeStruct((m, n), BF16),
        scratch_shapes=[pltpu.VMEM((tm, d), BF16)],
        compiler_params=pltpu.CompilerParams(
            dimension_semantics=("parallel", "arbitrary"), vmem_limit_bytes=_vmem_limit(vmem)),
        name="proj_in",
    )(x, gin, w, cos_t, sin_t)


def _attn_a_kernel(sink_ref, q_ref, kp_ref, kc_ref, kn_ref, vp_ref, vc_ref, vn_ref, gain_ref, o_ref, oacc_ref,
                   *, nb, n_kv, group):
    n = pl.program_id(1)
    blk = WINDOW
    rows = group * blk
    rr = lax.broadcasted_iota(jnp.int32, (rows, 3 * blk), 0) & (blk - 1)
    cc = lax.broadcasted_iota(jnp.int32, (rows, 3 * blk), 1)
    lo = jnp.where(n == 0, blk, 0)
    hi = jnp.where(n == nb - 1, 2 * blk, 3 * blk)
    valid = (cc >= rr) & (cc <= rr + 2 * WINDOW) & (cc >= lo) & (cc < hi)
    grp = lax.broadcasted_iota(jnp.int32, (rows, 1), 0) // blk
    for kv in range(n_kv):
        ks = slice(kv * HEAD_DIM, (kv + 1) * HEAD_DIM)
        k = jnp.concatenate([kp_ref[:, ks], kc_ref[:, ks], kn_ref[:, ks]], axis=0)
        v = jnp.concatenate([vp_ref[:, ks], vc_ref[:, ks], vn_ref[:, ks]], axis=0)
        q = jnp.concatenate(
            [q_ref[:, (kv * group + g) * HEAD_DIM:(kv * group + g + 1) * HEAD_DIM] for g in range(group)], axis=0)
        s = _dot_nt(q, k) * SCALE
        s = jnp.where(valid, s, NEG)
        sink = jnp.zeros((rows, 1), F32)
        for g in range(group):
            sink = jnp.where(grp == g, sink_ref[kv * group + g], sink)
        mx = jnp.maximum(jnp.max(s, axis=-1, keepdims=True), sink)
        p = jnp.exp(s - mx)
        den = jnp.sum(p, axis=-1, keepdims=True) + jnp.exp(sink - mx)
        o = _dot(p.astype(BF16), v) / den
        for g in range(group):
            h0 = (kv * group + g) * HEAD_DIM
            oacc_ref[:, h0:h0 + HEAD_DIM] = o[g * blk:(g + 1) * blk, :]
    o_ref[...] = _rms(oacc_ref[...], gain_ref[...]).astype(o_ref.dtype)


def _attn_a(proj, sink, gain, *, bsz, s_len, q_col, k_col, v_col, n_q, n_kv):
    m = proj.shape[0]
    blk = WINDOW
    nb = s_len // blk
    qw, kw = n_q * HEAD_DIM, n_kv * HEAD_DIM
    assert s_len % blk == 0 and q_col % qw == 0 and k_col % kw == 0 and v_col % kw == 0
    qb, kb, vb = q_col // qw, k_col // kw, v_col // kw

    def prev_map(col):
        return lambda b, n: (b * nb + jnp.maximum(n - 1, 0), col)

    def cur_map(col):
        return lambda b, n: (b * nb + n, col)

    def next_map(col):
        return lambda b, n: (b * nb + jnp.minimum(n + 1, nb - 1), col)

    return pl.pallas_call(
        functools.partial(_attn_a_kernel, nb=nb, n_kv=n_kv, group=n_q // n_kv),
        grid=(bsz, nb),
        in_specs=[
            pl.BlockSpec(memory_space=pltpu.SMEM),
            pl.BlockSpec((blk, qw), cur_map(qb)),
            pl.BlockSpec((blk, kw), prev_map(kb)),
            pl.BlockSpec((blk, kw), cur_map(kb)),
            pl.BlockSpec((blk, kw), next_map(kb)),
            pl.BlockSpec((blk, kw), prev_map(vb)),
            pl.BlockSpec((blk, kw), cur_map(vb)),
            pl.BlockSpec((blk, kw), next_map(vb)),
            pl.BlockSpec((1, qw), lambda b, n: (0, 0)),
        ],
        out_specs=pl.BlockSpec((blk, qw), lambda b, n: (b * nb + n, 0)),
        out_shape=jax.ShapeDtypeStruct((m, qw), BF16),
        scratch_shapes=[pltpu.VMEM((blk, qw), F32)],
        compiler_params=pltpu.CompilerParams(dimension_semantics=("parallel", "arbitrary")),
        name="attn_a",
    )(sink, proj, proj, proj, proj, proj, proj, proj, gain)


def _bias_tables(rel_bias):
    n_h, n_dr, _ = rel_bias.shape
    col = jnp.arange(GRID_W)
    col_start = jnp.clip(col - NA_KW // 2, 0, GRID_W - NA_KW)
    inside = (col[None, :] >= col_start[:, None]) & (col[None, :] < col_start[:, None] + NA_KW)
    dc = jnp.clip(col[None, :] - col[:, None] + (NA_KW - 1), 0, 2 * NA_KW - 2)
    tc = jnp.where(inside[None, None], rel_bias.astype(F32)[:, :, dc], NEG)
    pairs = jnp.concatenate([tc[:, :-1], tc[:, 1:]], axis=-1)
    return pairs.reshape(n_h * (n_dr - 1), GRID_W, 2 * GRID_W)


def _attn_b_kernel(q_ref, k_ref, v_ref, bias_ref, gain_ref, o_ref, orow_ref, s_ref, p_ref, *, rows, n_h, rpb):
    rb = pl.program_id(1)
    band0 = jnp.clip(rpb * rb - NA_KH // 2, 0, rows - 2 * rpb)
    n_pair = 2 * NA_KH - 2
    gain = gain_ref[...]

    def row_body(ri, carry):
        r = rpb * rb + ri
        r0 = jnp.clip(r - NA_KH // 2, 0, rows - NA_KH)
        off = pl.multiple_of((r0 - band0) * GRID_W, GRID_W)
        qoff = pl.multiple_of(ri * GRID_W, GRID_W)
        dr0 = r0 - r + (NA_KH - 1)
        for h in range(n_h):
            hs = slice(h * HEAD_DIM, (h + 1) * HEAD_DIM)
            q = q_ref[pl.ds(qoff, GRID_W), hs]
            kk = k_ref[pl.ds(off, NA_KH * GRID_W), hs]
            bias = jnp.concatenate([bias_ref[h * n_pair + dr0 + 2 * t] for t in range(NA_KH // 2)], axis=1)
            s_ref[h * GRID_W:(h + 1) * GRID_W, :] = _dot_nt(q, kk) * SCALE + bias
        s = s_ref[...]
        mx = jnp.max(s, axis=-1, keepdims=True)
        p = jnp.exp(s - mx)
        den = jnp.sum(p, axis=-1, keepdims=True)
        p_ref[...] = p.astype(BF16)
        for h in range(n_h):
            hs = slice(h * HEAD_DIM, (h + 1) * HEAD_DIM)
            rs = slice(h * GRID_W, (h + 1) * GRID_W)
            vv = v_ref[pl.ds(off, NA_KH * GRID_W), hs]
            orow_ref[:, hs] = _dot(p_ref[rs, :], vv) / den[rs]
        o_ref[pl.ds(qoff, GRID_W), :] = _rms(orow_ref[...], gain).astype(o_ref.dtype)
        return carry

    lax.fori_loop(0, rpb, row_body, 0)


def _attn_b(proj, bias_t, gain, *, bsz, s_len, q_col, k_col, v_col, n_h, rpb=8):
    m = proj.shape[0]
    rows = s_len // GRID_W
    width = n_h * HEAD_DIM
    assert rows % rpb == 0 and rows >= 2 * rpb and rpb >= NA_KH and q_col % width == 0
    nrb = rows // rpb
    band = 2 * rpb * GRID_W
    tq = rpb * GRID_W

    def band_map(col):
        def index(b, rb):
            start = jnp.clip(rpb * rb - NA_KH // 2, 0, rows - 2 * rpb)
            return (pl.multiple_of((b * rows + start) * GRID_W, GRID_W), col)
        return index

    keys = NA_KH * GRID_W
    vmem = (2 * (tq * width * 2 + 2 * band * width * 2 + tq * width * 2) + 2 * bias_t.size * 4 + GRID_W * width * 4
            + n_h * GRID_W * keys * 6)
    return pl.pallas_call(
        functools.partial(_attn_b_kernel, rows=rows, n_h=n_h, rpb=rpb),
        grid=(bsz, nrb),
        in_specs=[
            pl.BlockSpec((tq, width), lambda b, rb: (b * nrb + rb, q_col // width)),
            pl.BlockSpec((pl.Element(band), pl.Element(width)), band_map(k_col)),
            pl.BlockSpec((pl.Element(band), pl.Element(width)), band_map(v_col)),
            pl.BlockSpec(bias_t.shape, lambda b, rb: (0, 0, 0)),
            pl.BlockSpec((1, width), lambda b, rb: (0, 0)),
        ],
        out_specs=pl.BlockSpec((tq, width), lambda b, rb: (b * nrb + rb, 0)),
        out_shape=jax.ShapeDtypeStruct((m, width), BF16),
        scratch_shapes=[pltpu.VMEM((GRID_W, width), F32), pltpu.VMEM((n_h * GRID_W, keys), F32),
                        pltpu.VMEM((n_h * GRID_W, keys), BF16)],
        compiler_params=pltpu.CompilerParams(
            dimension_semantics=("parallel", "arbitrary"), vmem_limit_bytes=_vmem_limit(vmem)),
        name="attn_b",
    )(proj, proj, proj, bias_t, gain)


def _out_proj_kernel(a_ref, b_ref, wa_ref, wb_ref, x_ref, o_ref):
    o_ref[...] = x_ref[...] + (_dot(a_ref[...], wa_ref[...]) + _dot(b_ref[...], wb_ref[...]))


def _out_proj(oa, ob, w, x, *, tm=1024, tn=512):
    m, d = x.shape
    ka, kb = oa.shape[1], ob.shape[1]
    tm = min(tm, m)
    assert ka == kb and m % tm == 0 and d % tn == 0
    vmem = 2 * (2 * tm * ka * 2 + 2 * ka * tn * 2 + 2 * tm * tn * 4)
    return pl.pallas_call(
        _out_proj_kernel,
        grid=(m // tm, d // tn),
        in_specs=[
            pl.BlockSpec((tm, ka), lambda i, j: (i, 0)),
            pl.BlockSpec((tm, kb), lambda i, j: (i, 0)),
            pl.BlockSpec((ka, tn), lambda i, j: (0, j)),
            pl.BlockSpec((kb, tn), lambda i, j: (1, j)),
            pl.BlockSpec((tm, tn), lambda i, j: (i, j)),
        ],
        out_specs=pl.BlockSpec((tm, tn), lambda i, j: (i, j)),
        out_shape=jax.ShapeDtypeStruct((m, d), F32),
        compiler_params=pltpu.CompilerParams(
            dimension_semantics=("parallel", "arbitrary"), vmem_limit_bytes=_vmem_limit(vmem)),
        name="out_proj",
    )(oa, ob, w, w, x)


def _mem_kv_kernel(mem_ref, gain_ref, w_ref, o_ref, h_ref, *, tm):
    _norm_rows(mem_ref, gain_ref, h_ref, tm)
    o_ref[...] = _dot(h_ref[...], w_ref[...]).astype(o_ref.dtype)


def _mem_kv(mem, gain, w, *, tm=256):
    m, d = mem.shape
    n = w.shape[1]
    tm = min(tm, m)
    assert m % tm == 0
    vmem = 2 * tm * d * 4 + tm * d * 2 + 2 * d * n * 2 + 2 * tm * n * 2
    return pl.pallas_call(
        functools.partial(_mem_kv_kernel, tm=tm),
        grid=(m // tm,),
        in_specs=[
            pl.BlockSpec((tm, d), lambda i: (i, 0)),
            pl.BlockSpec((1, d), lambda i: (0, 0)),
            pl.BlockSpec((d, n), lambda i: (0, 0)),
        ],
        out_specs=pl.BlockSpec((tm, n), lambda i: (i, 0)),
        out_shape=jax.ShapeDtypeStruct((m, n), BF16),
        scratch_shapes=[pltpu.VMEM((tm, d), BF16)],
        compiler_params=pltpu.CompilerParams(
            dimension_semantics=("parallel",), vmem_limit_bytes=_vmem_limit(vmem)),
        name="mem_kv",
    )(mem, gain, w)


def _cross_kernel(x_ref, gain_ref, wq_ref, k_ref, v_ref, wo_ref, o_ref, h_ref, att_ref, *, tm):
    _norm_rows(x_ref, gain_ref, h_ref, tm)
    q = _dot(h_ref[...], wq_ref[...]).astype(BF16)
    for h in range(CA_HEADS):
        hs = slice(h * HEAD_DIM, (h + 1) * HEAD_DIM)
        s = _dot_nt(q[:, hs], k_ref[:, hs]) * SCALE
        mx = jnp.max(s, axis=-1, keepdims=True)
        p = jnp.exp(s - mx)
        den = jnp.sum(p, axis=-1, keepdims=True)
        att_ref[:, hs] = (_dot(p.astype(BF16), v_ref[:, hs]) / den).astype(BF16)
    o_ref[...] = x_ref[...] + _dot(att_ref[...], wo_ref[...])


def _cross(x, gain, wq, kv, wo, *, s_len, n_mem, tm=256):
    m, d = x.shape
    caw = wq.shape[1]
    tm = min(tm, s_len)
    assert m % tm == 0 and s_len % tm == 0 and caw == CA_HEADS * HEAD_DIM
    per_seq = s_len // tm
    vmem = 4 * tm * d * 4 + tm * d * 2 + 2 * 2 * d * caw * 2 + 4 * n_mem * caw * 2 + tm * caw * 2
    return pl.pallas_call(
        functools.partial(_cross_kernel, tm=tm),
        grid=(m // tm,),
        in_specs=[
            pl.BlockSpec((tm, d), lambda i: (i, 0)),
            pl.BlockSpec((1, d), lambda i: (0, 0)),
            pl.BlockSpec((d, caw), lambda i: (0, 0)),
            pl.BlockSpec((n_mem, caw), lambda i: (i // per_seq, 0)),
            pl.BlockSpec((n_mem, caw), lambda i: (i // per_seq, 1)),
            pl.BlockSpec((caw, d), lambda i: (0, 0)),
        ],
        out_specs=pl.BlockSpec((tm, d), lambda i: (i, 0)),
        out_shape=jax.ShapeDtypeStruct((m, d), F32),
        scratch_shapes=[pltpu.VMEM((tm, d), BF16), pltpu.VMEM((tm, caw), BF16)],
        compiler_params=pltpu.CompilerParams(
            dimension_semantics=("parallel",), vmem_limit_bytes=_vmem_limit(vmem)),
        name="cross",
    )(x, gain, wq, kv, kv, wo)


def _trunk(x3, mem3, p):
    bsz, s_len, d = x3.shape
    n_mem = mem3.shape[1]
    x = x3.reshape(bsz * s_len, d)
    mem = mem3.reshape(bsz * n_mem, d)
    cos_t, sin_t = _rope_tables(s_len)

    x = _ffn(x, p["ffn1_norm"], p["ffn1_w_gate"], p["ffn1_w_up"], p["ffn1_w_down"], p["final_norm"], final=False)
    proj = _proj_in(x, p["mix_norm"], p["w_in"], cos_t, sin_t, s_len=s_len, rope_ranges=p["rope_ranges"])
    c = p["cols"]
    oa = _attn_a(proj, p["a_sink"], p["a_out_norm"], bsz=bsz, s_len=s_len,
                 q_col=c["qa"], k_col=c["ka"], v_col=c["va"], n_q=p["a_q_heads"], n_kv=p["a_kv_heads"])
    ob = _attn_b(proj, p["b_bias"], p["b_out_norm"], bsz=bsz, s_len=s_len,
                 q_col=c["qb"], k_col=c["kb"], v_col=c["vb"], n_h=p["b_heads"])
    x = _out_proj(oa, ob, p["w_out"], x)
    kv = _mem_kv(mem, p["mem_norm"], p["ca_w_kv"])
    x = _cross(x, p["ca_norm"], p["ca_w_q"], kv, p["ca_w_o"], s_len=s_len, n_mem=n_mem)
    y = _ffn(x, p["ffn2_norm"], p["ffn2_w_gate"], p["ffn2_w_up"], p["ffn2_w_down"], p["final_norm"], final=True)
    return y.reshape(bsz, s_len, d)


def _prepare(ffn1_norm, ffn1_w_gate, ffn1_w_up, ffn1_w_down, mix_norm, w_in, a_sink, b_rel_bias, a_out_norm,
             b_out_norm, w_out, ca_norm, mem_norm, ca_w_q, ca_w_kv, ca_w_o, ffn2_norm, ffn2_w_gate, ffn2_w_up,
             ffn2_w_down, final_norm):
    d = w_in.shape[1]
    n_heads = d // HEAD_DIM
    a_q = n_heads // 2
    a_kv = a_q // 4
    b_h = n_heads - a_q
    aq_w, akv_w, b_w = a_q * HEAD_DIM, a_kv * HEAD_DIM, b_h * HEAD_DIM
    o0, o1, o2 = aq_w, aq_w + akv_w, aq_w + 2 * akv_w
    w = w_in[0]
    w_perm = jnp.concatenate([w[:, :o0], w[:, o2:], w[:, o0:o2]], axis=1).astype(BF16)
    cols = {"qa": 0, "qb": aq_w, "kb": aq_w + b_w, "vb": aq_w + 2 * b_w, "ka": aq_w + 3 * b_w,
            "va": aq_w + 3 * b_w + akv_w}

    def gain(g):
        return g.reshape(1, -1).astype(F32)

    def wt(a):
        return a[0].astype(BF16)

    return {
        "ffn1_norm": gain(ffn1_norm[0]), "ffn1_w_gate": wt(ffn1_w_gate), "ffn1_w_up": wt(ffn1_w_up),
        "ffn1_w_down": wt(ffn1_w_down),
        "mix_norm": gain(mix_norm[0]), "w_in": w_perm, "cols": cols,
        "rope_ranges": ((cols["qa"], cols["qa"] + aq_w), (cols["ka"], cols["ka"] + akv_w)),
        "a_sink": a_sink[0].astype(F32), "b_bias": _bias_tables(b_rel_bias[0]),
        "a_out_norm": gain(a_out_norm[0]), "b_out_norm": gain(b_out_norm[0]), "w_out": wt(w_out),
        "a_q_heads": a_q, "a_kv_heads": a_kv, "b_heads": b_h,
        "ca_norm": gain(ca_norm[0]), "mem_norm": gain(mem_norm[0]), "ca_w_q": wt(ca_w_q), "ca_w_kv": wt(ca_w_kv),
        "ca_w_o": wt(ca_w_o),
        "ffn2_norm": gain(ffn2_norm[0]), "ffn2_w_gate": wt(ffn2_w_gate), "ffn2_w_up": wt(ffn2_w_up),
        "ffn2_w_down": wt(ffn2_w_down), "final_norm": gain(final_norm),
    }


def kernel(x_prompt, x_sample, mem_prompt, mem_sample, ffn1_norm, ffn1_w_gate, ffn1_w_up, ffn1_w_down, mix_norm, w_in, a_sink, b_rel_bias, a_out_norm, b_out_norm, w_out, ca_norm, mem_norm, ca_w_q, ca_w_kv, ca_w_o, ffn2_norm, ffn2_w_gate, ffn2_w_up, ffn2_w_down, final_norm):
    assert ffn1_w_gate.shape[0] == 1, "single-layer trunk"
    p = _prepare(ffn1_norm, ffn1_w_gate, ffn1_w_up, ffn1_w_down, mix_norm, w_in, a_sink, b_rel_bias, a_out_norm,
                 b_out_norm, w_out, ca_norm, mem_norm, ca_w_q, ca_w_kv, ca_w_o, ffn2_norm, ffn2_w_gate, ffn2_w_up,
                 ffn2_w_down, final_norm)
    return (_trunk(x_prompt, mem_prompt, p), _trunk(x_sample, mem_sample, p))
```

```python
import functools

import jax
import jax.numpy as jnp
from jax import lax
from jax.experimental import pallas as pl
from jax.experimental.pallas import tpu as pltpu

F32 = jnp.float32
BF16 = jnp.bfloat16

HEAD_DIM = 128
WINDOW = 128
ROPE_THETA = 500000.0
ROPE_DIM = HEAD_DIM // 4
GRID_W = 64
NA_KH = 8
NA_KW = 16
CA_HEADS = 4
EPS = 1e-6
NEG = -1e30
SCALE = HEAD_DIM ** -0.5

V7X_VMEM_BYTES = 64 * 1024 * 1024
V7X_LANES = 128
NORM_ROWS = 16
NORM_UNROLL = 4


def _vmem_limit(nbytes):
    return int(min(nbytes + 12 * 1024 * 1024, V7X_VMEM_BYTES - 4 * 1024 * 1024))


def _rms(x, gain):
    ms = jnp.mean(x * x, axis=-1, keepdims=True)
    return x * lax.rsqrt(ms + EPS) * gain


def _norm_rows(src_ref, gain_ref, dst_ref, rows):
    gain = gain_ref[...]

    def body(i, carry):
        r = pl.multiple_of(i * NORM_ROWS, NORM_ROWS)
        x = src_ref[pl.ds(r, NORM_ROWS), :]
        dst_ref[pl.ds(r, NORM_ROWS), :] = _rms(x, gain).astype(dst_ref.dtype)
        return carry

    lax.fori_loop(0, rows // NORM_ROWS, body, 0, unroll=NORM_UNROLL)


def _dot(a, b):
    return jnp.dot(a, b, preferred_element_type=F32)


def _dot_nt(a, b):
    return lax.dot_general(a, b, (((1,), (1,)), ((), ())), preferred_element_type=F32)


def _ffn_kernel(x_hbm, gin_ref, *refs, tm, nf, cps, ncol, final):
    w_refs, (gfin_ref, o_ref, xbuf, h_ref, rs_ref, sem) = refs[:3 * cps], refs[3 * cps:]
    i = pl.program_id(0)
    s = pl.program_id(1)
    n_tiles = pl.num_programs(0)
    n_steps = pl.num_programs(1)
    d = o_ref.shape[1]

    def x_copy(tile):
        return pltpu.make_async_copy(x_hbm.at[pl.ds(pl.multiple_of(tile * tm, tm), tm), :], xbuf, sem)

    @pl.when(s == 0)
    def _():
        @pl.when(i == 0)
        def _():
            x_copy(0).start()

        x_copy(i).wait()
        gin = gin_ref[...]

        def first(j, carry):
            rows = pl.ds(pl.multiple_of(j * NORM_ROWS, NORM_ROWS), NORM_ROWS)
            x = xbuf[rows, :]
            h_ref[rows, :] = _rms(x, gin).astype(h_ref.dtype)
            o_ref[rows, :] = 2.0 * x
            return carry

        lax.fori_loop(0, tm // NORM_ROWS, first, 0, unroll=NORM_UNROLL)

        @pl.when(i + 1 < n_tiles)
        def _():
            x_copy(i + 1).start()

    def chunk(wg_ref, wu_ref, wd_ref):
        h = h_ref[...]
        g = _dot(h, wg_ref[...])
        u = _dot(h, wu_ref[...])
        act = (g / (1.0 + jnp.exp(-g)) * u).astype(BF16)
        for c in range(0, d, ncol):
            o_ref[:, c:c + ncol] += _dot(act, wd_ref[:, c:c + ncol])

    for c in range(cps):
        if (pl.cdiv(nf, cps) - 1) * cps + c < nf:
            chunk(*w_refs[3 * c:3 * c + 3])
        else:
            pl.when(s * cps + c < nf)(functools.partial(chunk, *w_refs[3 * c:3 * c + 3]))

    @pl.when(s == n_steps - 1)
    def _():
        def residual(i, carry):
            rows = pl.ds(pl.multiple_of(i * NORM_ROWS, NORM_ROWS), NORM_ROWS)
            y = 0.5 * o_ref[rows, :]
            o_ref[rows, :] = y
            if final:
                ms = jnp.mean(y * y, axis=-1, keepdims=True)
                rs_ref[rows, :] = jnp.broadcast_to(lax.rsqrt(ms + EPS), (NORM_ROWS, rs_ref.shape[1]))
            return carry

        lax.fori_loop(0, tm // NORM_ROWS, residual, 0, unroll=NORM_UNROLL)

        if final:
            gfin = gfin_ref[...]

            def scale(i, carry):
                rows = pl.ds(pl.multiple_of(i * NORM_ROWS, NORM_ROWS), NORM_ROWS)
                o_ref[rows, :] = o_ref[rows, :] * rs_ref[rows, :][:, :1] * gfin
                return carry

            lax.fori_loop(0, tm // NORM_ROWS, scale, 0, unroll=NORM_UNROLL)


def _ffn(x, gin, wg, wu, wd, gfin, *, final, tm=512, tf=256, cps=2, ncol=512):
    m, d = x.shape
    dff = wg.shape[1]
    tm = min(tm, m)
    nf = dff // tf
    cps = min(cps, nf)
    n_steps = pl.cdiv(nf, cps)
    assert m % tm == 0 and dff % tf == 0 and d % ncol == 0

    def chunk_specs(c):
        def col(i, s):
            return (0, jnp.minimum(s * cps + c, nf - 1))

        def row(i, s):
            return (jnp.minimum(s * cps + c, nf - 1), 0)

        return [pl.BlockSpec((d, tf), col), pl.BlockSpec((d, tf), col), pl.BlockSpec((tf, d), row)]

    w_specs = [spec for c in range(cps) for spec in chunk_specs(c)]
    vmem = tm * d * 4 + 2 * tm * d * 4 + tm * d * 2 + 2 * cps * 3 * d * tf * 2
    return pl.pallas_call(
        functools.partial(_ffn_kernel, tm=tm, nf=nf, cps=cps, ncol=ncol, final=final),
        grid=(m // tm, n_steps),
        in_specs=[
            pl.BlockSpec(memory_space=pl.ANY),
            pl.BlockSpec((1, d), lambda i, s: (0, 0)),
            *w_specs,
            pl.BlockSpec((1, d), lambda i, s: (0, 0)),
        ],
        out_specs=pl.BlockSpec((tm, d), lambda i, s: (i, 0)),
        out_shape=jax.ShapeDtypeStruct((m, d), F32),
        scratch_shapes=[pltpu.VMEM((tm, d), F32), pltpu.VMEM((tm, d), BF16), pltpu.VMEM((tm, V7X_LANES), F32),
                        pltpu.SemaphoreType.DMA],
        compiler_params=pltpu.CompilerParams(
            dimension_semantics=("arbitrary", "arbitrary"), vmem_limit_bytes=_vmem_limit(vmem)),
        name="ffn_final" if final else "ffn",
    )(x, gin, *([wg, wu, wd] * cps), gfin)


def _rope_tables(s_len):
    half = ROPE_DIM // 2
    inv = 1.0 / (ROPE_THETA ** (jnp.arange(half, dtype=F32) / half))
    ang = jnp.arange(s_len).astype(F32)[:, None] * inv[None, :]
    cos, sin = jnp.cos(ang), jnp.sin(ang)
    rest = HEAD_DIM - ROPE_DIM
    cos_t = jnp.concatenate([cos, cos, jnp.ones((s_len, rest), F32)], axis=1)
    sin_t = jnp.concatenate([-sin, sin, jnp.zeros((s_len, rest), F32)], axis=1)
    return cos_t, sin_t


def _proj_kernel(x_ref, gin_ref, w_ref, cos_ref, sin_ref, o_ref, h_ref, *, tm, tn, rope_ranges):
    j = pl.program_id(1)

    @pl.when(j == 0)
    def _():
        _norm_rows(x_ref, gin_ref, h_ref, tm)

    col = j * tn
    is_rope = functools.reduce(jnp.logical_or, [(col >= lo) & (col < hi) for lo, hi in rope_ranges])

    @pl.when(is_rope)
    def _():
        acc = _dot(h_ref[...], w_ref[...])
        cos = cos_ref[...]
        sin = sin_ref[...]
        lane = lax.broadcasted_iota(jnp.int32, (tm, HEAD_DIM), 1)
        half = ROPE_DIM // 2
        for c in range(0, tn, HEAD_DIM):
            xh = acc[:, c:c + HEAD_DIM]
            partner = jnp.where(lane < half, pltpu.roll(xh, HEAD_DIM - half, 1), pltpu.roll(xh, half, 1))
            rot = xh * cos + partner * sin
            o_ref[:, c:c + HEAD_DIM] = jnp.where(lane < ROPE_DIM, rot, xh).astype(o_ref.dtype)

    @pl.when(jnp.logical_not(is_rope))
    def _():
        o_ref[...] = _dot(h_ref[...], w_ref[...]).astype(o_ref.dtype)


def _proj_in(x, gin, w, cos_t, sin_t, *, s_len, rope_ranges, tm=1024, tn=512):
    m, d = x.shape
    n = w.shape[1]
    tm = min(tm, s_len)
    assert m % tm == 0 and s_len % tm == 0 and n % tn == 0
    assert all(lo % tn == 0 and hi % tn == 0 for lo, hi in rope_ranges)
    per_seq = s_len // tm
    vmem = 2 * tm * d * 4 + tm * d * 2 + 2 * d * tn * 2 + 2 * tm * tn * 2 + 4 * tm * HEAD_DIM * 4
    return pl.pallas_call(
        functools.partial(_proj_kernel, tm=tm, tn=tn, rope_ranges=rope_ranges),
        grid=(m // tm, n // tn),
        in_specs=[
            pl.BlockSpec((tm, d), lambda i, j: (i, 0)),
            pl.BlockSpec((1, d), lambda i, j: (0, 0)),
            pl.BlockSpec((d, tn), lambda i, j: (0, j)),
            pl.BlockSpec((tm, HEAD_DIM), lambda i, j: (i % per_seq, 0)),
            pl.BlockSpec((tm, HEAD_DIM), lambda i, j: (i % per_seq, 0)),
        ],
        out_specs=pl.BlockSpec((tm, tn), lambda i, j: (i, j)),
        out_shape=jax.ShapeDtypeStruct((m, n), BF16),
        scratch_shapes=[pltpu.VMEM((tm, d), BF16)],
        compiler_params=pltpu.CompilerParams(
            dimension_semantics=("parallel", "arbitrary"), vmem_limit_bytes=_vmem_limit(vmem)),
        name="proj_in",
    )(x, gin, w, cos_t, sin_t)


def _attn_a_kernel(sink_ref, q_ref, kp_ref, kc_ref, kn_ref, vp_ref, vc_ref, vn_ref, gain_ref, o_ref, oacc_ref,
                   *, nb, n_kv, group):
    n = pl.program_id(1)
    blk = WINDOW
    rows = group * blk
    rr = lax.broadcasted_iota(jnp.int32, (rows, 3 * blk), 0) & (blk - 1)
    cc = lax.broadcasted_iota(jnp.int32, (rows, 3 * blk), 1)
    lo = jnp.where(n == 0, blk, 0)
    hi = jnp.where(n == nb - 1, 2 * blk, 3 * blk)
    valid = (cc >= rr) & (cc <= rr + 2 * WINDOW) & (cc >= lo) & (cc < hi)
    grp = lax.broadcasted_iota(jnp.int32, (rows, 1), 0) // blk
    for kv in range(n_kv):
        ks = slice(kv * HEAD_DIM, (kv + 1) * HEAD_DIM)
        k = jnp.concatenate([kp_ref[:, ks], kc_ref[:, ks], kn_ref[:, ks]], axis=0)
        v = jnp.concatenate([vp_ref[:, ks], vc_ref[:, ks], vn_ref[:, ks]], axis=0)
        q = jnp.concatenate(
            [q_ref[:, (kv * group + g) * HEAD_DIM:(kv * group + g + 1) * HEAD_DIM] for g in range(group)], axis=0)
        s = _dot_nt(q, k) * SCALE
        s = jnp.where(valid, s, NEG)
        sink = jnp.zeros((rows, 1), F32)
        for g in range(group):
            sink = jnp.where(grp == g, sink_ref[kv * group + g], sink)
        mx = jnp.maximum(jnp.max(s, axis=-1, keepdims=True), sink)
        p = jnp.exp(s - mx)
        den = jnp.sum(p, axis=-1, keepdims=True) + jnp.exp(sink - mx)
        o = _dot(p.astype(BF16), v) / den
        for g in range(group):
            h0 = (kv * group + g) * HEAD_DIM
            oacc_ref[:, h0:h0 + HEAD_DIM] = o[g * blk:(g + 1) * blk, :]
    o_ref[...] = _rms(oacc_ref[...], gain_ref[...]).astype(o_ref.dtype)


def _attn_a(proj, sink, gain, *, bsz, s_len, q_col, k_col, v_col, n_q, n_kv):
    m = proj.shape[0]
    blk = WINDOW
    nb = s_len // blk
    qw, kw = n_q * HEAD_DIM, n_kv * HEAD_DIM
    assert s_len % blk == 0 and q_col % qw == 0 and k_col % kw == 0 and v_col % kw == 0
    qb, kb, vb = q_col // qw, k_col // kw, v_col // kw

    def prev_map(col):
        return lambda b, n: (b * nb + jnp.maximum(n - 1, 0), col)

    def cur_map(col):
        return lambda b, n: (b * nb + n, col)

    def next_map(col):
        return lambda b, n: (b * nb + jnp.minimum(n + 1, nb - 1), col)

    return pl.pallas_call(
        functools.partial(_attn_a_kernel, nb=nb, n_kv=n_kv, group=n_q // n_kv),
        grid=(bsz, nb),
        in_specs=[
            pl.BlockSpec(memory_space=pltpu.SMEM),
            pl.BlockSpec((blk, qw), cur_map(qb)),
            pl.BlockSpec((blk, kw), prev_map(kb)),
            pl.BlockSpec((blk, kw), cur_map(kb)),
            pl.BlockSpec((blk, kw), next_map(kb)),
            pl.BlockSpec((blk, kw), prev_map(vb)),
            pl.BlockSpec((blk, kw), cur_map(vb)),
            pl.BlockSpec((blk, kw), next_map(vb)),
            pl.BlockSpec((1, qw), lambda b, n: (0, 0)),
        ],
        out_specs=pl.BlockSpec((blk, qw), lambda b, n: (b * nb + n, 0)),
        out_shape=jax.ShapeDtypeStruct((m, qw), BF16),
        scratch_shapes=[pltpu.VMEM((blk, qw), F32)],
        compiler_params=pltpu.CompilerParams(dimension_semantics=("parallel", "arbitrary")),
        name="attn_a",
    )(sink, proj, proj, proj, proj, proj, proj, proj, gain)


def _bias_tables(rel_bias):
    n_h, n_dr, _ = rel_bias.shape
    col = jnp.arange(GRID_W)
    col_start = jnp.clip(col - NA_KW // 2, 0, GRID_W - NA_KW)
    inside = (col[None, :] >= col_start[:, None]) & (col[None, :] < col_start[:, None] + NA_KW)
    dc = jnp.clip(col[None, :] - col[:, None] + (NA_KW - 1), 0, 2 * NA_KW - 2)
    tc = jnp.where(inside[None, None], rel_bias.astype(F32)[:, :, dc], NEG)
    pairs = jnp.concatenate([tc[:, :-1], tc[:, 1:]], axis=-1)
    return pairs.reshape(n_h * (n_dr - 1), GRID_W, 2 * GRID_W)


def _attn_b_kernel(q_ref, k_ref, v_ref, bias_ref, gain_ref, o_ref, orow_ref, s_ref, p_ref, *, rows, n_h, rpb):
    rb = pl.program_id(1)
    band0 = jnp.clip(rpb * rb - NA_KH // 2, 0, rows - 2 * rpb)
    n_pair = 2 * NA_KH - 2
    gain = gain_ref[...]

    def row_body(ri, carry):
        r = rpb * rb + ri
        r0 = jnp.clip(r - NA_KH // 2, 0, rows - NA_KH)
        off = pl.multiple_of((r0 - band0) * GRID_W, GRID_W)
        qoff = pl.multiple_of(ri * GRID_W, GRID_W)
        dr0 = r0 - r + (NA_KH - 1)
        for h in range(n_h):
            hs = slice(h * HEAD_DIM, (h + 1) * HEAD_DIM)
            q = q_ref[pl.ds(qoff, GRID_W), hs]
            kk = k_ref[pl.ds(off, NA_KH * GRID_W), hs]
            bias = jnp.concatenate([bias_ref[h * n_pair + dr0 + 2 * t] for t in range(NA_KH // 2)], axis=1)
            s_ref[h * GRID_W:(h + 1) * GRID_W, :] = _dot_nt(q, kk) * SCALE + bias
        s = s_ref[...]
        mx = jnp.max(s, axis=-1, keepdims=True)
        p = jnp.exp(s - mx)
        den = jnp.sum(p, axis=-1, keepdims=True)
        p_ref[...] = p.astype(BF16)
        for h in range(n_h):
            hs = slice(h * HEAD_DIM, (h + 1) * HEAD_DIM)
            rs = slice(h * GRID_W, (h + 1) * GRID_W)
            vv = v_ref[pl.ds(off, NA_KH * GRID_W), hs]
            orow_ref[:, hs] = _dot(p_ref[rs, :], vv) / den[rs]
        o_ref[pl.ds(qoff, GRID_W), :] = _rms(orow_ref[...], gain).astype(o_ref.dtype)
        return carry

    lax.fori_loop(0, rpb, row_body, 0)


def _attn_b(proj, bias_t, gain, *, bsz, s_len, q_col, k_col, v_col, n_h, rpb=8):
    m = proj.shape[0]
    rows = s_len // GRID_W
    width = n_h * HEAD_DIM
    assert rows % rpb == 0 and rows >= 2 * rpb and rpb >= NA_KH
    assert all(c % V7X_LANES == 0 for c in (q_col, k_col, v_col))
    nrb = rows // rpb
    band = 2 * rpb * GRID_W
    tq = rpb * GRID_W

    def band_map(col):
        def index(b, rb):
            start = jnp.clip(rpb * rb - NA_KH // 2, 0, rows - 2 * rpb)
            return (pl.multiple_of((b * rows + start) * GRID_W, GRID_W), col)
        return index

    keys = NA_KH * GRID_W
    vmem = (2 * (tq * width * 2 + 2 * band * width * 2 + tq * width * 2) + 2 * bias_t.size * 4 + GRID_W * width * 4
            + n_h * GRID_W * keys * 6)
    return pl.pallas_call(
        functools.partial(_attn_b_kernel, rows=rows, n_h=n_h, rpb=rpb),
        grid=(bsz, nrb),
        in_specs=[
            pl.BlockSpec((pl.Element(tq), pl.Element(width)),
                         lambda b, rb: (pl.multiple_of((b * nrb + rb) * tq, tq), q_col)),
            pl.BlockSpec((pl.Element(band), pl.Element(width)), band_map(k_col)),
            pl.BlockSpec((pl.Element(band), pl.Element(width)), band_map(v_col)),
            pl.BlockSpec(bias_t.shape, lambda b, rb: (0, 0, 0)),
            pl.BlockSpec((1, width), lambda b, rb: (0, 0)),
        ],
        out_specs=pl.BlockSpec((tq, width), lambda b, rb: (b * nrb + rb, 0)),
        out_shape=jax.ShapeDtypeStruct((m, width), BF16),
        scratch_shapes=[pltpu.VMEM((GRID_W, width), F32), pltpu.VMEM((n_h * GRID_W, keys), F32),
                        pltpu.VMEM((n_h * GRID_W, keys), BF16)],
        compiler_params=pltpu.CompilerParams(
            dimension_semantics=("parallel", "arbitrary"), vmem_limit_bytes=_vmem_limit(vmem)),
        name="attn_b",
    )(proj, proj, proj, bias_t, gain)


def _out_proj_kernel(a_ref, b_ref, wa_ref, wb_ref, x_ref, o_ref):
    o_ref[...] = x_ref[...] + (_dot(a_ref[...], wa_ref[...]) + _dot(b_ref[...], wb_ref[...]))


def _out_proj(oa, ob, w, x, *, tm=1024, tn=512):
    m, d = x.shape
    ka, kb = oa.shape[1], ob.shape[1]
    tm = min(tm, m)
    assert ka == kb and m % tm == 0 and d % tn == 0
    vmem = 2 * (2 * tm * ka * 2 + 2 * ka * tn * 2 + 2 * tm * tn * 4)
    return pl.pallas_call(
        _out_proj_kernel,
        grid=(m // tm, d // tn),
        in_specs=[
            pl.BlockSpec((tm, ka), lambda i, j: (i, 0)),
            pl.BlockSpec((tm, kb), lambda i, j: (i, 0)),
            pl.BlockSpec((ka, tn), lambda i, j: (0, j)),
            pl.BlockSpec((kb, tn), lambda i, j: (1, j)),
            pl.BlockSpec((tm, tn), lambda i, j: (i, j)),
        ],
        out_specs=pl.BlockSpec((tm, tn), lambda i, j: (i, j)),
        out_shape=jax.ShapeDtypeStruct((m, d), F32),
        compiler_params=pltpu.CompilerParams(
            dimension_semantics=("parallel", "arbitrary"), vmem_limit_bytes=_vmem_limit(vmem)),
        name="out_proj",
    )(oa, ob, w, w, x)


def _mem_kv_kernel(mem_ref, gain_ref, w_ref, o_ref, h_ref, *, tm):
    _norm_rows(mem_ref, gain_ref, h_ref, tm)
    o_ref[...] = _dot(h_ref[...], w_ref[...]).astype(o_ref.dtype)


def _mem_kv(mem, gain, w, *, tm=256):
    m, d = mem.shape
    n = w.shape[1]
    tm = min(tm, m)
    assert m % tm == 0
    vmem = 2 * tm * d * 4 + tm * d * 2 + 2 * d * n * 2 + 2 * tm * n * 2
    return pl.pallas_call(
        functools.partial(_mem_kv_kernel, tm=tm),
        grid=(m // tm,),
        in_specs=[
            pl.BlockSpec((tm, d), lambda i: (i, 0)),
            pl.BlockSpec((1, d), lambda i: (0, 0)),
            pl.BlockSpec((d, n), lambda i: (0, 0)),
        ],
        out_specs=pl.BlockSpec((tm, n), lambda i: (i, 0)),
        out_shape=jax.ShapeDtypeStruct((m, n), BF16),
        scratch_shapes=[pltpu.VMEM((tm, d), BF16)],
        compiler_params=pltpu.CompilerParams(
            dimension_semantics=("parallel",), vmem_limit_bytes=_vmem_limit(vmem)),
        name="mem_kv",
    )(mem, gain, w)


def _cross_kernel(x_ref, gain_ref, wq_ref, k_ref, v_ref, wo_ref, o_ref, h_ref, att_ref, *, tm):
    _norm_rows(x_ref, gain_ref, h_ref, tm)
    q = _dot(h_ref[...], wq_ref[...]).astype(BF16)
    for h in range(CA_HEADS):
        hs = slice(h * HEAD_DIM, (h + 1) * HEAD_DIM)
        s = _dot_nt(q[:, hs], k_ref[:, hs]) * SCALE
        mx = jnp.max(s, axis=-1, keepdims=True)
        p = jnp.exp(s - mx)
        den = jnp.sum(p, axis=-1, keepdims=True)
        att_ref[:, hs] = (_dot(p.astype(BF16), v_ref[:, hs]) / den).astype(BF16)
    o_ref[...] = x_ref[...] + _dot(att_ref[...], wo_ref[...])


def _cross(x, gain, wq, kv, wo, *, s_len, n_mem, tm=256):
    m, d = x.shape
    caw = wq.shape[1]
    tm = min(tm, s_len)
    assert m % tm == 0 and s_len % tm == 0 and caw == CA_HEADS * HEAD_DIM
    per_seq = s_len // tm
    vmem = 4 * tm * d * 4 + tm * d * 2 + 2 * 2 * d * caw * 2 + 4 * n_mem * caw * 2 + tm * caw * 2
    return pl.pallas_call(
        functools.partial(_cross_kernel, tm=tm),
        grid=(m // tm,),
        in_specs=[
            pl.BlockSpec((tm, d), lambda i: (i, 0)),
            pl.BlockSpec((1, d), lambda i: (0, 0)),
            pl.BlockSpec((d, caw), lambda i: (0, 0)),
            pl.BlockSpec((n_mem, caw), lambda i: (i // per_seq, 0)),
            pl.BlockSpec((n_mem, caw), lambda i: (i // per_seq, 1)),
            pl.BlockSpec((caw, d), lambda i: (0, 0)),
        ],
        out_specs=pl.BlockSpec((tm, d), lambda i: (i, 0)),
        out_shape=jax.ShapeDtypeStruct((m, d), F32),
        scratch_shapes=[pltpu.VMEM((tm, d), BF16), pltpu.VMEM((tm, caw), BF16)],
        compiler_params=pltpu.CompilerParams(
            dimension_semantics=("parallel",), vmem_limit_bytes=_vmem_limit(vmem)),
        name="cross",
    )(x, gain, wq, kv, kv, wo)


def _trunk(x3, mem3, p):
    bsz, s_len, d = x3.shape
    n_mem = mem3.shape[1]
    x = x3.reshape(bsz * s_len, d)
    mem = mem3.reshape(bsz * n_mem, d)
    cos_t, sin_t = _rope_tables(s_len)

    x = _ffn(x, p["ffn1_norm"], p["ffn1_w_gate"], p["ffn1_w_up"], p["ffn1_w_down"], p["final_norm"], final=False)
    proj = _proj_in(x, p["mix_norm"], p["w_in"], cos_t, sin_t, s_len=s_len, rope_ranges=p["rope_ranges"])
    c = p["cols"]
    oa = _attn_a(proj, p["a_sink"], p["a_out_norm"], bsz=bsz, s_len=s_len,
                 q_col=c["qa"], k_col=c["ka"], v_col=c["va"], n_q=p["a_q_heads"], n_kv=p["a_kv_heads"])
    ob = _attn_b(proj, p["b_bias"], p["b_out_norm"], bsz=bsz, s_len=s_len,
                 q_col=c["qb"], k_col=c["kb"], v_col=c["vb"], n_h=p["b_heads"])
    x = _out_proj(oa, ob, p["w_out"], x)
    kv = _mem_kv(mem, p["mem_norm"], p["ca_w_kv"])
    x = _cross(x, p["ca_norm"], p["ca_w_q"], kv, p["ca_w_o"], s_len=s_len, n_mem=n_mem)
    y = _ffn(x, p["ffn2_norm"], p["ffn2_w_gate"], p["ffn2_w_up"], p["ffn2_w_down"], p["final_norm"], final=True)
    return y.reshape(bsz, s_len, d)


def _prepare(ffn1_norm, ffn1_w_gate, ffn1_w_up, ffn1_w_down, mix_norm, w_in, a_sink, b_rel_bias, a_out_norm,
             b_out_norm, w_out, ca_norm, mem_norm, ca_w_q, ca_w_kv, ca_w_o, ffn2_norm, ffn2_w_gate, ffn2_w_up,
             ffn2_w_down, final_norm):
    d = w_in.shape[1]
    n_heads = d // HEAD_DIM
    a_q = n_heads // 2
    a_kv = a_q // 4
    b_h = n_heads - a_q
    aq_w, akv_w, b_w = a_q * HEAD_DIM, a_kv * HEAD_DIM, b_h * HEAD_DIM
    cols = {"qa": 0, "ka": aq_w, "va": aq_w + akv_w, "qb": aq_w + 2 * akv_w, "kb": aq_w + 2 * akv_w + b_w,
            "vb": aq_w + 2 * akv_w + 2 * b_w}

    def gain(g):
        return g.reshape(1, -1).astype(F32)

    def wt(a):
        return a[0].astype(BF16)

    return {
        "ffn1_norm": gain(ffn1_norm[0]), "ffn1_w_gate": wt(ffn1_w_gate), "ffn1_w_up": wt(ffn1_w_up),
        "ffn1_w_down": wt(ffn1_w_down),
        "mix_norm": gain(mix_norm[0]), "w_in": wt(w_in), "cols": cols,
        "rope_ranges": ((cols["qa"], cols["qa"] + aq_w), (cols["ka"], cols["ka"] + akv_w)),
        "a_sink": a_sink[0].astype(F32), "b_bias": _bias_tables(b_rel_bias[0]),
        "a_out_norm": gain(a_out_norm[0]), "b_out_norm": gain(b_out_norm[0]), "w_out": wt(w_out),
        "a_q_heads": a_q, "a_kv_heads": a_kv, "b_heads": b_h,
        "ca_norm": gain(ca_norm[0]), "mem_norm": gain(mem_norm[0]), "ca_w_q": wt(ca_w_q), "ca_w_kv": wt(ca_w_kv),
        "ca_w_o": wt(ca_w_o),
        "ffn2_norm": gain(ffn2_norm[0]), "ffn2_w_gate": wt(ffn2_w_gate), "ffn2_w_up": wt(ffn2_w_up),
        "ffn2_w_down": wt(ffn2_w_down), "final_norm": gain(final_norm),
    }


def kernel(x_prompt, x_sample, mem_prompt, mem_sample, ffn1_norm, ffn1_w_gate, ffn1_w_up, ffn1_w_down, mix_norm, w_in, a_sink, b_rel_bias, a_out_norm, b_out_norm, w_out, ca_norm, mem_norm, ca_w_q, ca_w_kv, ca_w_o, ffn2_norm, ffn2_w_gate, ffn2_w_up, ffn2_w_down, final_norm):
    assert ffn1_w_gate.shape[0] == 1, "single-layer trunk"
    p = _prepare(ffn1_norm, ffn1_w_gate, ffn1_w_up, ffn1_w_down, mix_norm, w_in, a_sink, b_rel_bias, a_out_norm,
                 b_out_norm, w_out, ca_norm, mem_norm, ca_w_q, ca_w_kv, ca_w_o, ffn2_norm, ffn2_w_gate, ffn2_w_up,
                 ffn2_w_down, final_norm)
    return (_trunk(x_prompt, mem_prompt, p), _trunk(x_sample, mem_sample, p))
```

```python
import functools

import jax
import jax.numpy as jnp
from jax import lax
from jax.experimental import pallas as pl
from jax.experimental.pallas import tpu as pltpu

F32 = jnp.float32
BF16 = jnp.bfloat16

HEAD_DIM = 128
WINDOW = 128
ROPE_THETA = 500000.0
ROPE_DIM = HEAD_DIM // 4
GRID_W = 64
NA_KH = 8
NA_KW = 16
CA_HEADS = 4
EPS = 1e-6
NEG = -1e30
SCALE = HEAD_DIM ** -0.5
LOG2E = 1.4426950408889634

V7X_VMEM_BYTES = 64 * 1024 * 1024
V7X_LANES = 128
NORM_ROWS = 16
NORM_UNROLL = 4


def _vmem_limit(nbytes):
    return int(min(nbytes + 12 * 1024 * 1024, V7X_VMEM_BYTES - 4 * 1024 * 1024))


def _rms(x, gain):
    ms = jnp.mean(x * x, axis=-1, keepdims=True)
    return x * lax.rsqrt(ms + EPS) * gain


def _norm_rows(src_ref, gain_ref, dst_ref, rows):
    gain = gain_ref[...]

    def body(i, carry):
        r = pl.multiple_of(i * NORM_ROWS, NORM_ROWS)
        x = src_ref[pl.ds(r, NORM_ROWS), :]
        dst_ref[pl.ds(r, NORM_ROWS), :] = _rms(x, gain).astype(dst_ref.dtype)
        return carry

    lax.fori_loop(0, rows // NORM_ROWS, body, 0, unroll=NORM_UNROLL)


def _dot(a, b):
    return jnp.dot(a, b, preferred_element_type=F32)


def _dot_nt(a, b):
    return lax.dot_general(a, b, (((1,), (1,)), ((), ())), preferred_element_type=F32)


def _ffn_kernel(x_hbm, gin_ref, *refs, tm, nf, cps, ncol, final):
    w_refs, (gfin_ref, o_ref, xbuf, h_ref, rs_ref, sem) = refs[:3 * cps], refs[3 * cps:]
    i = pl.program_id(0)
    s = pl.program_id(1)
    n_tiles = pl.num_programs(0)
    n_steps = pl.num_programs(1)
    d = o_ref.shape[1]

    def x_copy(tile):
        return pltpu.make_async_copy(x_hbm.at[pl.ds(pl.multiple_of(tile * tm, tm), tm), :], xbuf, sem)

    @pl.when(s == 0)
    def _():
        @pl.when(i == 0)
        def _():
            x_copy(0).start()

        x_copy(i).wait()
        gin = gin_ref[...]

        def first(j, carry):
            rows = pl.ds(pl.multiple_of(j * NORM_ROWS, NORM_ROWS), NORM_ROWS)
            x = xbuf[rows, :]
            h_ref[rows, :] = _rms(x, gin).astype(h_ref.dtype)
            o_ref[rows, :] = x
            return carry

        lax.fori_loop(0, tm // NORM_ROWS, first, 0, unroll=NORM_UNROLL)

        @pl.when(i + 1 < n_tiles)
        def _():
            x_copy(i + 1).start()

    def chunk(wg_ref, wu_ref, wd_ref):
        h = h_ref[...]
        g = _dot(h, wg_ref[...])
        u = _dot(h, wu_ref[...])
        act = (0.5 * (g / (1.0 + jnp.exp(-g)) * u)).astype(BF16)
        for c in range(0, d, ncol):
            o_ref[:, c:c + ncol] += _dot(act, wd_ref[:, c:c + ncol])

    for c in range(cps):
        if (pl.cdiv(nf, cps) - 1) * cps + c < nf:
            chunk(*w_refs[3 * c:3 * c + 3])
        else:
            pl.when(s * cps + c < nf)(functools.partial(chunk, *w_refs[3 * c:3 * c + 3]))

    if final:
        @pl.when(s == n_steps - 1)
        def _():
            def row_scale(j, carry):
                rows = pl.ds(pl.multiple_of(j * NORM_ROWS, NORM_ROWS), NORM_ROWS)
                y = o_ref[rows, :]
                ms = jnp.mean(y * y, axis=-1, keepdims=True)
                rs_ref[rows, :] = jnp.broadcast_to(lax.rsqrt(ms + EPS), (NORM_ROWS, rs_ref.shape[1]))
                return carry

            lax.fori_loop(0, tm // NORM_ROWS, row_scale, 0, unroll=NORM_UNROLL)
            gfin = gfin_ref[...]

            def scale(j, carry):
                rows = pl.ds(pl.multiple_of(j * NORM_ROWS, NORM_ROWS), NORM_ROWS)
                o_ref[rows, :] = o_ref[rows, :] * rs_ref[rows, :][:, :1] * gfin
                return carry

            lax.fori_loop(0, tm // NORM_ROWS, scale, 0, unroll=NORM_UNROLL)


def _ffn(x, gin, wg, wu, wd, gfin, *, final, tm=512, tf=256, cps=2, ncol=512):
    m, d = x.shape
    dff = wg.shape[1]
    tm = min(tm, m)
    nf = dff // tf
    cps = min(cps, nf)
    n_steps = pl.cdiv(nf, cps)
    assert m % tm == 0 and dff % tf == 0 and d % ncol == 0

    def chunk_specs(c):
        def col(i, s):
            return (0, jnp.minimum(s * cps + c, nf - 1))

        def row(i, s):
            return (jnp.minimum(s * cps + c, nf - 1), 0)

        return [pl.BlockSpec((d, tf), col), pl.BlockSpec((d, tf), col), pl.BlockSpec((tf, d), row)]

    w_specs = [spec for c in range(cps) for spec in chunk_specs(c)]
    vmem = tm * d * 4 + 2 * tm * d * 4 + tm * d * 2 + 2 * cps * 3 * d * tf * 2
    return pl.pallas_call(
        functools.partial(_ffn_kernel, tm=tm, nf=nf, cps=cps, ncol=ncol, final=final),
        grid=(m // tm, n_steps),
        in_specs=[
            pl.BlockSpec(memory_space=pl.ANY),
            pl.BlockSpec((1, d), lambda i, s: (0, 0)),
            *w_specs,
            pl.BlockSpec((1, d), lambda i, s: (0, 0)),
        ],
        out_specs=pl.BlockSpec((tm, d), lambda i, s: (i, 0)),
        out_shape=jax.ShapeDtypeStruct((m, d), F32),
        scratch_shapes=[pltpu.VMEM((tm, d), F32), pltpu.VMEM((tm, d), BF16), pltpu.VMEM((tm, V7X_LANES), F32),
                        pltpu.SemaphoreType.DMA],
        compiler_params=pltpu.CompilerParams(
            dimension_semantics=("arbitrary", "arbitrary"), vmem_limit_bytes=_vmem_limit(vmem)),
        name="ffn_final" if final else "ffn",
    )(x, gin, *([wg, wu, wd] * cps), gfin)


def _rope_tables(s_len):
    half = ROPE_DIM // 2
    inv = 1.0 / (ROPE_THETA ** (jnp.arange(half, dtype=F32) / half))
    ang = jnp.arange(s_len).astype(F32)[:, None] * inv[None, :]
    cos, sin = jnp.cos(ang), jnp.sin(ang)
    rest = HEAD_DIM - ROPE_DIM
    cos_t = jnp.concatenate([cos, cos, jnp.ones((s_len, rest), F32)], axis=1)
    sin_t = jnp.concatenate([-sin, sin, jnp.zeros((s_len, rest), F32)], axis=1)
    return cos_t, sin_t


def _proj_kernel(x_ref, gin_ref, w_ref, cos_ref, sin_ref, o_ref, h_ref, *, tm, tn, rope_ranges):
    j = pl.program_id(1)

    @pl.when(j == 0)
    def _():
        _norm_rows(x_ref, gin_ref, h_ref, tm)

    col = j * tn
    is_rope = functools.reduce(jnp.logical_or, [(col >= lo) & (col < hi) for lo, hi in rope_ranges])

    @pl.when(is_rope)
    def _():
        acc = _dot(h_ref[...], w_ref[...])
        cos = cos_ref[...]
        sin = sin_ref[...]
        lane = lax.broadcasted_iota(jnp.int32, (tm, HEAD_DIM), 1)
        half = ROPE_DIM // 2
        for c in range(0, tn, HEAD_DIM):
            xh = acc[:, c:c + HEAD_DIM]
            partner = jnp.where(lane < half, pltpu.roll(xh, HEAD_DIM - half, 1), pltpu.roll(xh, half, 1))
            rot = xh * cos + partner * sin
            o_ref[:, c:c + HEAD_DIM] = jnp.where(lane < ROPE_DIM, rot, xh).astype(o_ref.dtype)

    @pl.when(jnp.logical_not(is_rope))
    def _():
        o_ref[...] = _dot(h_ref[...], w_ref[...]).astype(o_ref.dtype)


def _proj_in(x, gin, w, cos_t, sin_t, *, s_len, rope_ranges, tm=1024, tn=512):
    m, d = x.shape
    n = w.shape[1]
    tm = min(tm, s_len)
    assert m % tm == 0 and s_len % tm == 0 and n % tn == 0
    assert all(lo % tn == 0 and hi % tn == 0 for lo, hi in rope_ranges)
    per_seq = s_len // tm
    vmem = 2 * tm * d * 4 + tm * d * 2 + 2 * d * tn * 2 + 2 * tm * tn * 2 + 4 * tm * HEAD_DIM * 4
    return pl.pallas_call(
        functools.partial(_proj_kernel, tm=tm, tn=tn, rope_ranges=rope_ranges),
        grid=(m // tm, n // tn),
        in_specs=[
            pl.BlockSpec((tm, d), lambda i, j: (i, 0)),
            pl.BlockSpec((1, d), lambda i, j: (0, 0)),
            pl.BlockSpec((d, tn), lambda i, j: (0, j)),
            pl.BlockSpec((tm, HEAD_DIM), lambda i, j: (i % per_seq, 0)),
            pl.BlockSpec((tm, HEAD_DIM), lambda i, j: (i % per_seq, 0)),
        ],
        out_specs=pl.BlockSpec((tm, tn), lambda i, j: (i, j)),
        out_shape=jax.ShapeDtypeStruct((m, n), BF16),
        scratch_shapes=[pltpu.VMEM((tm, d), BF16)],
        compiler_params=pltpu.CompilerParams(
            dimension_semantics=("parallel", "arbitrary"), vmem_limit_bytes=_vmem_limit(vmem)),
        name="proj_in",
    )(x, gin, w, cos_t, sin_t)


def _attn_a_kernel(sink_ref, q_ref, kp_ref, kc_ref, kn_ref, vp_ref, vc_ref, vn_ref, gain_ref, o_ref, oacc_ref,
                   *, nb, n_kv, group):
    n = pl.program_id(1)
    blk = WINDOW
    rows = group * blk
    rr = lax.broadcasted_iota(jnp.int32, (rows, 3 * blk), 0) & (blk - 1)
    cc = lax.broadcasted_iota(jnp.int32, (rows, 3 * blk), 1)
    lo = jnp.where(n == 0, blk, 0)
    hi = jnp.where(n == nb - 1, 2 * blk, 3 * blk)
    valid = (cc >= rr) & (cc <= rr + 2 * WINDOW) & (cc >= lo) & (cc < hi)
    grp = lax.broadcasted_iota(jnp.int32, (rows, 1), 0) // blk
    for kv in range(n_kv):
        ks = slice(kv * HEAD_DIM, (kv + 1) * HEAD_DIM)
        k = jnp.concatenate([kp_ref[:, ks], kc_ref[:, ks], kn_ref[:, ks]], axis=0)
        v = jnp.concatenate([vp_ref[:, ks], vc_ref[:, ks], vn_ref[:, ks]], axis=0)
        q = jnp.concatenate(
            [q_ref[:, (kv * group + g) * HEAD_DIM:(kv * group + g + 1) * HEAD_DIM] for g in range(group)], axis=0)
        s = _dot_nt(q, k) * (SCALE * LOG2E)
        s = jnp.where(valid, s, NEG)
        sink = jnp.zeros((rows, 1), F32)
        for g in range(group):
            sink = jnp.where(grp == g, sink_ref[kv * group + g] * LOG2E, sink)
        mx = jnp.maximum(jnp.max(s, axis=-1, keepdims=True), sink)
        p = jnp.exp2(s - mx)
        den = jnp.sum(p, axis=-1, keepdims=True) + jnp.exp2(sink - mx)
        o = _dot(p.astype(BF16), v) / den
        for g in range(group):
            h0 = (kv * group + g) * HEAD_DIM
            oacc_ref[:, h0:h0 + HEAD_DIM] = o[g * blk:(g + 1) * blk, :]
    o_ref[...] = _rms(oacc_ref[...], gain_ref[...]).astype(o_ref.dtype)


def _attn_a(proj, sink, gain, *, bsz, s_len, q_col, k_col, v_col, n_q, n_kv):
    m = proj.shape[0]
    blk = WINDOW
    nb = s_len // blk
    qw, kw = n_q * HEAD_DIM, n_kv * HEAD_DIM
    assert s_len % blk == 0 and q_col % qw == 0 and k_col % kw == 0 and v_col % kw == 0
    qb, kb, vb = q_col // qw, k_col // kw, v_col // kw

    def prev_map(col):
        return lambda b, n: (b * nb + jnp.maximum(n - 1, 0), col)

    def cur_map(col):
        return lambda b, n: (b * nb + n, col)

    def next_map(col):
        return lambda b, n: (b * nb + jnp.minimum(n + 1, nb - 1), col)

    return pl.pallas_call(
        functools.partial(_attn_a_kernel, nb=nb, n_kv=n_kv, group=n_q // n_kv),
        grid=(bsz, nb),
        in_specs=[
            pl.BlockSpec(memory_space=pltpu.SMEM),
            pl.BlockSpec((blk, qw), cur_map(qb)),
            pl.BlockSpec((blk, kw), prev_map(kb)),
            pl.BlockSpec((blk, kw), cur_map(kb)),
            pl.BlockSpec((blk, kw), next_map(kb)),
            pl.BlockSpec((blk, kw), prev_map(vb)),
            pl.BlockSpec((blk, kw), cur_map(vb)),
            pl.BlockSpec((blk, kw), next_map(vb)),
            pl.BlockSpec((1, qw), lambda b, n: (0, 0)),
        ],
        out_specs=pl.BlockSpec((blk, qw), lambda b, n: (b * nb + n, 0)),
        out_shape=jax.ShapeDtypeStruct((m, qw), BF16),
        scratch_shapes=[pltpu.VMEM((blk, qw), F32)],
        compiler_params=pltpu.CompilerParams(dimension_semantics=("parallel", "arbitrary")),
        name="attn_a",
    )(sink, proj, proj, proj, proj, proj, proj, proj, gain)


def _bias_tables(rel_bias):
    n_h, n_dr, _ = rel_bias.shape
    col = jnp.arange(GRID_W)
    col_start = jnp.clip(col - NA_KW // 2, 0, GRID_W - NA_KW)
    inside = (col[None, :] >= col_start[:, None]) & (col[None, :] < col_start[:, None] + NA_KW)
    dc = jnp.clip(col[None, :] - col[:, None] + (NA_KW - 1), 0, 2 * NA_KW - 2)
    tc = jnp.where(inside[None, None], rel_bias.astype(F32)[:, :, dc], NEG)
    pairs = jnp.concatenate([tc[:, :-1], tc[:, 1:]], axis=-1)
    return pairs.reshape(n_h * (n_dr - 1), GRID_W, 2 * GRID_W) * LOG2E


def _attn_b_kernel(q_ref, k_ref, v_ref, bias_ref, gain_ref, o_ref, orow_ref, s_ref, p_ref, *, rows, n_h, rpb):
    rb = pl.program_id(1)
    band0 = jnp.clip(rpb * rb - NA_KH // 2, 0, rows - 2 * rpb)
    n_pair = 2 * NA_KH - 2
    gain = gain_ref[...]

    def row_body(ri, carry):
        r = rpb * rb + ri
        r0 = jnp.clip(r - NA_KH // 2, 0, rows - NA_KH)
        off = pl.multiple_of((r0 - band0) * GRID_W, GRID_W)
        qoff = pl.multiple_of(ri * GRID_W, GRID_W)
        dr0 = r0 - r + (NA_KH - 1)
        for h in range(n_h):
            hs = slice(h * HEAD_DIM, (h + 1) * HEAD_DIM)
            q = q_ref[pl.ds(qoff, GRID_W), hs]
            kk = k_ref[pl.ds(off, NA_KH * GRID_W), hs]
            bias = jnp.concatenate([bias_ref[h * n_pair + dr0 + 2 * t] for t in range(NA_KH // 2)], axis=1)
            s_ref[h * GRID_W:(h + 1) * GRID_W, :] = _dot_nt(q, kk) * (SCALE * LOG2E) + bias
        s = s_ref[...]
        mx = jnp.max(s, axis=-1, keepdims=True)
        p = jnp.exp2(s - mx)
        den = jnp.sum(p, axis=-1, keepdims=True)
        p_ref[...] = p.astype(BF16)
        for h in range(n_h):
            hs = slice(h * HEAD_DIM, (h + 1) * HEAD_DIM)
            rs = slice(h * GRID_W, (h + 1) * GRID_W)
            vv = v_ref[pl.ds(off, NA_KH * GRID_W), hs]
            orow_ref[:, hs] = _dot(p_ref[rs, :], vv) / den[rs]
        o_ref[pl.ds(qoff, GRID_W), :] = _rms(orow_ref[...], gain).astype(o_ref.dtype)
        return carry

    lax.fori_loop(0, rpb, row_body, 0)


def _attn_b(proj, bias_t, gain, *, bsz, s_len, q_col, k_col, v_col, n_h, rpb=8):
    m = proj.shape[0]
    rows = s_len // GRID_W
    width = n_h * HEAD_DIM
    assert rows % rpb == 0 and rows >= 2 * rpb and rpb >= NA_KH
    assert all(c % V7X_LANES == 0 for c in (q_col, k_col, v_col))
    nrb = rows // rpb
    band = 2 * rpb * GRID_W
    tq = rpb * GRID_W

    def band_map(col):
        def index(b, rb):
            start = jnp.clip(rpb * rb - NA_KH // 2, 0, rows - 2 * rpb)
            return (pl.multiple_of((b * rows + start) * GRID_W, GRID_W), col)
        return index

    keys = NA_KH * GRID_W
    vmem = (2 * (tq * width * 2 + 2 * band * width * 2 + tq * width * 2) + 2 * bias_t.size * 4 + GRID_W * width * 4
            + n_h * GRID_W * keys * 6)
    return pl.pallas_call(
        functools.partial(_attn_b_kernel, rows=rows, n_h=n_h, rpb=rpb),
        grid=(bsz, nrb),
        in_specs=[
            pl.BlockSpec((pl.Element(tq), pl.Element(width)),
                         lambda b, rb: (pl.multiple_of((b * nrb + rb) * tq, tq), q_col)),
            pl.BlockSpec((pl.Element(band), pl.Element(width)), band_map(k_col)),
            pl.BlockSpec((pl.Element(band), pl.Element(width)), band_map(v_col)),
            pl.BlockSpec(bias_t.shape, lambda b, rb: (0, 0, 0)),
            pl.BlockSpec((1, width), lambda b, rb: (0, 0)),
        ],
        out_specs=pl.BlockSpec((tq, width), lambda b, rb: (b * nrb + rb, 0)),
        out_shape=jax.ShapeDtypeStruct((m, width), BF16),
        scratch_shapes=[pltpu.VMEM((GRID_W, width), F32), pltpu.VMEM((n_h * GRID_W, keys), F32),
                        pltpu.VMEM((n_h * GRID_W, keys), BF16)],
        compiler_params=pltpu.CompilerParams(
            dimension_semantics=("parallel", "arbitrary"), vmem_limit_bytes=_vmem_limit(vmem)),
        name="attn_b",
    )(proj, proj, proj, bias_t, gain)


def _out_proj_kernel(a_ref, b_ref, wa_ref, wb_ref, x_ref, o_ref):
    o_ref[...] = x_ref[...] + (_dot(a_ref[...], wa_ref[...]) + _dot(b_ref[...], wb_ref[...]))


def _out_proj(oa, ob, w, x, *, tm=1024, tn=512):
    m, d = x.shape
    ka, kb = oa.shape[1], ob.shape[1]
    tm = min(tm, m)
    assert ka == kb and m % tm == 0 and d % tn == 0
    vmem = 2 * (2 * tm * ka * 2 + 2 * ka * tn * 2 + 2 * tm * tn * 4)
    return pl.pallas_call(
        _out_proj_kernel,
        grid=(m // tm, d // tn),
        in_specs=[
            pl.BlockSpec((tm, ka), lambda i, j: (i, 0)),
            pl.BlockSpec((tm, kb), lambda i, j: (i, 0)),
            pl.BlockSpec((ka, tn), lambda i, j: (0, j)),
            pl.BlockSpec((kb, tn), lambda i, j: (1, j)),
            pl.BlockSpec((tm, tn), lambda i, j: (i, j)),
        ],
        out_specs=pl.BlockSpec((tm, tn), lambda i, j: (i, j)),
        out_shape=jax.ShapeDtypeStruct((m, d), F32),
        compiler_params=pltpu.CompilerParams(
            dimension_semantics=("parallel", "arbitrary"), vmem_limit_bytes=_vmem_limit(vmem)),
        name="out_proj",
    )(oa, ob, w, w, x)


def _mem_kv_kernel(mem_ref, gain_ref, w_ref, o_ref, h_ref, *, tm):
    _norm_rows(mem_ref, gain_ref, h_ref, tm)
    o_ref[...] = _dot(h_ref[...], w_ref[...]).astype(o_ref.dtype)


def _mem_kv(mem, gain, w, *, tm=256):
    m, d = mem.shape
    n = w.shape[1]
    tm = min(tm, m)
    assert m % tm == 0
    vmem = 2 * tm * d * 4 + tm * d * 2 + 2 * d * n * 2 + 2 * tm * n * 2
    return pl.pallas_call(
        functools.partial(_mem_kv_kernel, tm=tm),
        grid=(m // tm,),
        in_specs=[
            pl.BlockSpec((tm, d), lambda i: (i, 0)),
            pl.BlockSpec((1, d), lambda i: (0, 0)),
            pl.BlockSpec((d, n), lambda i: (0, 0)),
        ],
        out_specs=pl.BlockSpec((tm, n), lambda i: (i, 0)),
        out_shape=jax.ShapeDtypeStruct((m, n), BF16),
        scratch_shapes=[pltpu.VMEM((tm, d), BF16)],
        compiler_params=pltpu.CompilerParams(
            dimension_semantics=("parallel",), vmem_limit_bytes=_vmem_limit(vmem)),
        name="mem_kv",
    )(mem, gain, w)


def _cross_kernel(x_ref, gain_ref, wq_ref, k_ref, v_ref, wo_ref, o_ref, h_ref, att_ref, *, tm):
    _norm_rows(x_ref, gain_ref, h_ref, tm)
    q = _dot(h_ref[...], wq_ref[...]).astype(BF16)
    for h in range(CA_HEADS):
        hs = slice(h * HEAD_DIM, (h + 1) * HEAD_DIM)
        s = _dot_nt(q[:, hs], k_ref[:, hs]) * (SCALE * LOG2E)
        mx = jnp.max(s, axis=-1, keepdims=True)
        p = jnp.exp2(s - mx)
        den = jnp.sum(p, axis=-1, keepdims=True)
        att_ref[:, hs] = (_dot(p.astype(BF16), v_ref[:, hs]) / den).astype(BF16)
    o_ref[...] = x_ref[...] + _dot(att_ref[...], wo_ref[...])


def _cross(x, gain, wq, kv, wo, *, s_len, n_mem, tm=512):
    m, d = x.shape
    caw = wq.shape[1]
    tm = min(tm, s_len)
    assert m % tm == 0 and s_len % tm == 0 and caw == CA_HEADS * HEAD_DIM
    per_seq = s_len // tm
    vmem = 4 * tm * d * 4 + tm * d * 2 + 2 * 2 * d * caw * 2 + 4 * n_mem * caw * 2 + tm * caw * 2
    return pl.pallas_call(
        functools.partial(_cross_kernel, tm=tm),
        grid=(m // tm,),
        in_specs=[
            pl.BlockSpec((tm, d), lambda i: (i, 0)),
            pl.BlockSpec((1, d), lambda i: (0, 0)),
            pl.BlockSpec((d, caw), lambda i: (0, 0)),
            pl.BlockSpec((n_mem, caw), lambda i: (i // per_seq, 0)),
            pl.BlockSpec((n_mem, caw), lambda i: (i // per_seq, 1)),
            pl.BlockSpec((caw, d), lambda i: (0, 0)),
        ],
        out_specs=pl.BlockSpec((tm, d), lambda i: (i, 0)),
        out_shape=jax.ShapeDtypeStruct((m, d), F32),
        scratch_shapes=[pltpu.VMEM((tm, d), BF16), pltpu.VMEM((tm, caw), BF16)],
        compiler_params=pltpu.CompilerParams(
            dimension_semantics=("parallel",), vmem_limit_bytes=_vmem_limit(vmem)),
        name="cross",
    )(x, gain, wq, kv, kv, wo)


def _trunk(x3, mem3, p):
    bsz, s_len, d = x3.shape
    n_mem = mem3.shape[1]
    x = x3.reshape(bsz * s_len, d)
    mem = mem3.reshape(bsz * n_mem, d)
    cos_t, sin_t = _rope_tables(s_len)

    x = _ffn(x, p["ffn1_norm"], p["ffn1_w_gate"], p["ffn1_w_up"], p["ffn1_w_down"], p["final_norm"], final=False)
    proj = _proj_in(x, p["mix_norm"], p["w_in"], cos_t, sin_t, s_len=s_len, rope_ranges=p["rope_ranges"])
    c = p["cols"]
    oa = _attn_a(proj, p["a_sink"], p["a_out_norm"], bsz=bsz, s_len=s_len,
                 q_col=c["qa"], k_col=c["ka"], v_col=c["va"], n_q=p["a_q_heads"], n_kv=p["a_kv_heads"])
    ob = _attn_b(proj, p["b_bias"], p["b_out_norm"], bsz=bsz, s_len=s_len,
                 q_col=c["qb"], k_col=c["kb"], v_col=c["vb"], n_h=p["b_heads"])
    x = _out_proj(oa, ob, p["w_out"], x)
    kv = _mem_kv(mem, p["mem_norm"], p["ca_w_kv"])
    x = _cross(x, p["ca_norm"], p["ca_w_q"], kv, p["ca_w_o"], s_len=s_len, n_mem=n_mem)
    y = _ffn(x, p["ffn2_norm"], p["ffn2_w_gate"], p["ffn2_w_up"], p["ffn2_w_down"], p["final_norm"], final=True)
    return y.reshape(bsz, s_len, d)


def _prepare(ffn1_norm, ffn1_w_gate, ffn1_w_up, ffn1_w_down, mix_norm, w_in, a_sink, b_rel_bias, a_out_norm,
             b_out_norm, w_out, ca_norm, mem_norm, ca_w_q, ca_w_kv, ca_w_o, ffn2_norm, ffn2_w_gate, ffn2_w_up,
             ffn2_w_down, final_norm):
    d = w_in.shape[1]
    n_heads = d // HEAD_DIM
    a_q = n_heads // 2
    a_kv = a_q // 4
    b_h = n_heads - a_q
    aq_w, akv_w, b_w = a_q * HEAD_DIM, a_kv * HEAD_DIM, b_h * HEAD_DIM
    cols = {"qa": 0, "ka": aq_w, "va": aq_w + akv_w, "qb": aq_w + 2 * akv_w, "kb": aq_w + 2 * akv_w + b_w,
            "vb": aq_w + 2 * akv_w + 2 * b_w}

    def gain(g):
        return g.reshape(1, -1).astype(F32)

    def wt(a):
        return a[0].astype(BF16)

    return {
        "ffn1_norm": gain(ffn1_norm[0]), "ffn1_w_gate": wt(ffn1_w_gate), "ffn1_w_up": wt(ffn1_w_up),
        "ffn1_w_down": wt(ffn1_w_down),
        "mix_norm": gain(mix_norm[0]), "w_in": wt(w_in), "cols": cols,
        "rope_ranges": ((cols["qa"], cols["qa"] + aq_w), (cols["ka"], cols["ka"] + akv_w)),
        "a_sink": a_sink[0].astype(F32), "b_bias": _bias_tables(b_rel_bias[0]),
        "a_out_norm": gain(a_out_norm[0]), "b_out_norm": gain(b_out_norm[0]), "w_out": wt(w_out),
        "a_q_heads": a_q, "a_kv_heads": a_kv, "b_heads": b_h,
        "ca_norm": gain(ca_norm[0]), "mem_norm": gain(mem_norm[0]), "ca_w_q": wt(ca_w_q), "ca_w_kv": wt(ca_w_kv),
        "ca_w_o": wt(ca_w_o),
        "ffn2_norm": gain(ffn2_norm[0]), "ffn2_w_gate": wt(ffn2_w_gate), "ffn2_w_up": wt(ffn2_w_up),
        "ffn2_w_down": wt(ffn2_w_down), "final_norm": gain(final_norm),
    }


def kernel(x_prompt, x_sample, mem_prompt, mem_sample, ffn1_norm, ffn1_w_gate, ffn1_w_up, ffn1_w_down, mix_norm, w_in, a_sink, b_rel_bias, a_out_norm, b_out_norm, w_out, ca_norm, mem_norm, ca_w_q, ca_w_kv, ca_w_o, ffn2_norm, ffn2_w_gate, ffn2_w_up, ffn2_w_down, final_norm):
    assert ffn1_w_gate.shape[0] == 1, "single-layer trunk"
    p = _prepare(ffn1_norm, ffn1_w_gate, ffn1_w_up, ffn1_w_down, mix_norm, w_in, a_sink, b_rel_bias, a_out_norm,
                 b_out_norm, w_out, ca_norm, mem_norm, ca_w_q, ca_w_kv, ca_w_o, ffn2_norm, ffn2_w_gate, ffn2_w_up,
                 ffn2_w_down, final_norm)
    return (_trunk(x_prompt, mem_prompt, p), _trunk(x_sample, mem_sample, p))
```

```python
import functools

import jax
import jax.numpy as jnp
from jax import lax
from jax.experimental import pallas as pl
from jax.experimental.pallas import tpu as pltpu

F32 = jnp.float32
BF16 = jnp.bfloat16

HEAD_DIM = 128
WINDOW = 128
ROPE_THETA = 500000.0
ROPE_DIM = HEAD_DIM // 4
GRID_W = 64
NA_KH = 8
NA_KW = 16
CA_HEADS = 4
EPS = 1e-6
NEG = -1e30
SCALE = HEAD_DIM ** -0.5
LOG2E = 1.4426950408889634

V7X_VMEM_BYTES = 64 * 1024 * 1024
V7X_LANES = 128
NORM_ROWS = 16
NORM_UNROLL = 4


def _vmem_limit(nbytes):
    return int(min(nbytes + 12 * 1024 * 1024, V7X_VMEM_BYTES - 4 * 1024 * 1024))


def _rms(x, gain):
    ms = jnp.mean(x * x, axis=-1, keepdims=True)
    return x * lax.rsqrt(ms + EPS) * gain


def _norm_rows(src_ref, gain_ref, dst_ref, rows):
    gain = gain_ref[...]

    def body(i, carry):
        r = pl.multiple_of(i * NORM_ROWS, NORM_ROWS)
        x = src_ref[pl.ds(r, NORM_ROWS), :]
        dst_ref[pl.ds(r, NORM_ROWS), :] = _rms(x, gain).astype(dst_ref.dtype)
        return carry

    lax.fori_loop(0, rows // NORM_ROWS, body, 0, unroll=NORM_UNROLL)


def _dot(a, b):
    return jnp.dot(a, b, preferred_element_type=F32)


def _dot_nt(a, b):
    return lax.dot_general(a, b, (((1,), (1,)), ((), ())), preferred_element_type=F32)


def _ffn_kernel(x_hbm, gin_ref, *refs, tm, nf, cps, ncol, final):
    w_refs, (gfin_ref, o_ref, xbuf, h_ref, rs_ref, sem) = refs[:3 * cps], refs[3 * cps:]
    i = pl.program_id(0)
    s = pl.program_id(1)
    n_tiles = pl.num_programs(0)
    n_steps = pl.num_programs(1)
    d = o_ref.shape[1]

    def x_copy(tile):
        return pltpu.make_async_copy(x_hbm.at[pl.ds(pl.multiple_of(tile * tm, tm), tm), :], xbuf, sem)

    @pl.when(s == 0)
    def _():
        @pl.when(i == 0)
        def _():
            x_copy(0).start()

        x_copy(i).wait()
        gin = gin_ref[...]

        def first(j, carry):
            rows = pl.ds(pl.multiple_of(j * NORM_ROWS, NORM_ROWS), NORM_ROWS)
            x = xbuf[rows, :]
            h_ref[rows, :] = _rms(x, gin).astype(h_ref.dtype)
            o_ref[rows, :] = x
            return carry

        lax.fori_loop(0, tm // NORM_ROWS, first, 0, unroll=NORM_UNROLL)

        @pl.when(i + 1 < n_tiles)
        def _():
            x_copy(i + 1).start()

    def chunk(wg_ref, wu_ref, wd_ref):
        h = h_ref[...]
        g = _dot(h, wg_ref[...])
        u = _dot(h, wu_ref[...])
        act = (0.5 * (g / (1.0 + jnp.exp(-g)) * u)).astype(BF16)
        for c in range(0, d, ncol):
            o_ref[:, c:c + ncol] += _dot(act, wd_ref[:, c:c + ncol])

    def chunks(count):
        for c in range(count):
            chunk(*w_refs[3 * c:3 * c + 3])

    tail = nf - (pl.cdiv(nf, cps) - 1) * cps
    if tail == cps:
        chunks(cps)
    else:
        pl.when(s < n_steps - 1)(functools.partial(chunks, cps))
        pl.when(s == n_steps - 1)(functools.partial(chunks, tail))

    if final:
        @pl.when(s == n_steps - 1)
        def _():
            def row_scale(j, carry):
                rows = pl.ds(pl.multiple_of(j * NORM_ROWS, NORM_ROWS), NORM_ROWS)
                y = o_ref[rows, :]
                ms = jnp.mean(y * y, axis=-1, keepdims=True)
                rs_ref[rows, :] = jnp.broadcast_to(lax.rsqrt(ms + EPS), (NORM_ROWS, rs_ref.shape[1]))
                return carry

            lax.fori_loop(0, tm // NORM_ROWS, row_scale, 0, unroll=NORM_UNROLL)
            gfin = gfin_ref[...]

            def scale(j, carry):
                rows = pl.ds(pl.multiple_of(j * NORM_ROWS, NORM_ROWS), NORM_ROWS)
                o_ref[rows, :] = o_ref[rows, :] * rs_ref[rows, :][:, :1] * gfin
                return carry

            lax.fori_loop(0, tm // NORM_ROWS, scale, 0, unroll=NORM_UNROLL)


def _ffn(x, gin, wg, wu, wd, gfin, *, final, tm=512, tf=256, cps=2, ncol=512):
    m, d = x.shape
    dff = wg.shape[1]
    tm = min(tm, m)
    nf = dff // tf
    cps = min(cps, nf)
    n_steps = pl.cdiv(nf, cps)
    assert m % tm == 0 and dff % tf == 0 and d % ncol == 0

    def chunk_specs(c):
        def col(i, s):
            return (0, jnp.minimum(s * cps + c, nf - 1))

        def row(i, s):
            return (jnp.minimum(s * cps + c, nf - 1), 0)

        return [pl.BlockSpec((d, tf), col), pl.BlockSpec((d, tf), col), pl.BlockSpec((tf, d), row)]

    w_specs = [spec for c in range(cps) for spec in chunk_specs(c)]
    vmem = tm * d * 4 + 2 * tm * d * 4 + tm * d * 2 + 2 * cps * 3 * d * tf * 2
    return pl.pallas_call(
        functools.partial(_ffn_kernel, tm=tm, nf=nf, cps=cps, ncol=ncol, final=final),
        grid=(m // tm, n_steps),
        in_specs=[
            pl.BlockSpec(memory_space=pl.ANY),
            pl.BlockSpec((1, d), lambda i, s: (0, 0)),
            *w_specs,
            pl.BlockSpec((1, d), lambda i, s: (0, 0)),
        ],
        out_specs=pl.BlockSpec((tm, d), lambda i, s: (i, 0)),
        out_shape=jax.ShapeDtypeStruct((m, d), F32),
        scratch_shapes=[pltpu.VMEM((tm, d), F32), pltpu.VMEM((tm, d), BF16), pltpu.VMEM((tm, V7X_LANES), F32),
                        pltpu.SemaphoreType.DMA],
        compiler_params=pltpu.CompilerParams(
            dimension_semantics=("arbitrary", "arbitrary"), vmem_limit_bytes=_vmem_limit(vmem)),
        name="ffn_final" if final else "ffn",
    )(x, gin, *([wg, wu, wd] * cps), gfin)


def _rope_tables(s_len):
    half = ROPE_DIM // 2
    inv = 1.0 / (ROPE_THETA ** (jnp.arange(half, dtype=F32) / half))
    ang = jnp.arange(s_len).astype(F32)[:, None] * inv[None, :]
    cos, sin = jnp.cos(ang), jnp.sin(ang)
    rest = HEAD_DIM - ROPE_DIM
    cos_t = jnp.concatenate([cos, cos, jnp.ones((s_len, rest), F32)], axis=1)
    sin_t = jnp.concatenate([-sin, sin, jnp.zeros((s_len, rest), F32)], axis=1)
    return cos_t, sin_t


def _proj_kernel(x_ref, gin_ref, w_ref, cos_ref, sin_ref, o_ref, h_ref, *, tm, tn, rope_ranges):
    j = pl.program_id(1)

    @pl.when(j == 0)
    def _():
        _norm_rows(x_ref, gin_ref, h_ref, tm)

    col = j * tn
    is_rope = functools.reduce(jnp.logical_or, [(col >= lo) & (col < hi) for lo, hi in rope_ranges])

    @pl.when(is_rope)
    def _():
        acc = _dot(h_ref[...], w_ref[...])
        cos = cos_ref[...]
        sin = sin_ref[...]
        lane = lax.broadcasted_iota(jnp.int32, (tm, HEAD_DIM), 1)
        half = ROPE_DIM // 2
        for c in range(0, tn, HEAD_DIM):
            xh = acc[:, c:c + HEAD_DIM]
            partner = jnp.where(lane < half, pltpu.roll(xh, HEAD_DIM - half, 1), pltpu.roll(xh, half, 1))
            rot = xh * cos + partner * sin
            o_ref[:, c:c + HEAD_DIM] = jnp.where(lane < ROPE_DIM, rot, xh).astype(o_ref.dtype)

    @pl.when(jnp.logical_not(is_rope))
    def _():
        o_ref[...] = _dot(h_ref[...], w_ref[...]).astype(o_ref.dtype)


def _proj_in(x, gin, w, cos_t, sin_t, *, s_len, rope_ranges, tm=1024, tn=512):
    m, d = x.shape
    n = w.shape[1]
    tm = min(tm, s_len)
    assert m % tm == 0 and s_len % tm == 0 and n % tn == 0
    assert all(lo % tn == 0 and hi % tn == 0 for lo, hi in rope_ranges)
    per_seq = s_len // tm
    vmem = 2 * tm * d * 4 + tm * d * 2 + 2 * d * tn * 2 + 2 * tm * tn * 2 + 4 * tm * HEAD_DIM * 4
    return pl.pallas_call(
        functools.partial(_proj_kernel, tm=tm, tn=tn, rope_ranges=rope_ranges),
        grid=(m // tm, n // tn),
        in_specs=[
            pl.BlockSpec((tm, d), lambda i, j: (i, 0)),
            pl.BlockSpec((1, d), lambda i, j: (0, 0)),
            pl.BlockSpec((d, tn), lambda i, j: (0, j)),
            pl.BlockSpec((tm, HEAD_DIM), lambda i, j: (i % per_seq, 0)),
            pl.BlockSpec((tm, HEAD_DIM), lambda i, j: (i % per_seq, 0)),
        ],
        out_specs=pl.BlockSpec((tm, tn), lambda i, j: (i, j)),
        out_shape=jax.ShapeDtypeStruct((m, n), BF16),
        scratch_shapes=[pltpu.VMEM((tm, d), BF16)],
        compiler_params=pltpu.CompilerParams(
            dimension_semantics=("parallel", "arbitrary"), vmem_limit_bytes=_vmem_limit(vmem)),
        name="proj_in",
    )(x, gin, w, cos_t, sin_t)


def _attn_a_kernel(sink_ref, q_ref, kp_ref, kc_ref, kn_ref, vp_ref, vc_ref, vn_ref, gain_ref, o_ref, oacc_ref,
                   *, nb, n_kv, group):
    n = pl.program_id(1)
    blk = WINDOW
    rows = group * blk
    rr = lax.broadcasted_iota(jnp.int32, (rows, 3 * blk), 0) & (blk - 1)
    cc = lax.broadcasted_iota(jnp.int32, (rows, 3 * blk), 1)
    lo = jnp.where(n == 0, blk, 0)
    hi = jnp.where(n == nb - 1, 2 * blk, 3 * blk)
    valid = (cc >= rr) & (cc <= rr + 2 * WINDOW) & (cc >= lo) & (cc < hi)
    grp = lax.broadcasted_iota(jnp.int32, (rows, 1), 0) // blk
    for kv in range(n_kv):
        ks = slice(kv * HEAD_DIM, (kv + 1) * HEAD_DIM)
        k = jnp.concatenate([kp_ref[:, ks], kc_ref[:, ks], kn_ref[:, ks]], axis=0)
        v = jnp.concatenate([vp_ref[:, ks], vc_ref[:, ks], vn_ref[:, ks]], axis=0)
        q = jnp.concatenate(
            [q_ref[:, (kv * group + g) * HEAD_DIM:(kv * group + g + 1) * HEAD_DIM] for g in range(group)], axis=0)
        s = _dot_nt(q, k) * (SCALE * LOG2E)
        s = jnp.where(valid, s, NEG)
        sink = jnp.zeros((rows, 1), F32)
        for g in range(group):
            sink = jnp.where(grp == g, sink_ref[kv * group + g] * LOG2E, sink)
        mx = jnp.maximum(jnp.max(s, axis=-1, keepdims=True), sink)
        p = jnp.exp2(s - mx)
        den = jnp.sum(p, axis=-1, keepdims=True) + jnp.exp2(sink - mx)
        o = _dot(p.astype(BF16), v) / den
        for g in range(group):
            h0 = (kv * group + g) * HEAD_DIM
            oacc_ref[:, h0:h0 + HEAD_DIM] = o[g * blk:(g + 1) * blk, :]
    o_ref[...] = _rms(oacc_ref[...], gain_ref[...]).astype(o_ref.dtype)


def _attn_a(proj, sink, gain, *, bsz, s_len, q_col, k_col, v_col, n_q, n_kv):
    m = proj.shape[0]
    blk = WINDOW
    nb = s_len // blk
    qw, kw = n_q * HEAD_DIM, n_kv * HEAD_DIM
    assert s_len % blk == 0 and q_col % qw == 0 and k_col % kw == 0 and v_col % kw == 0
    qb, kb, vb = q_col // qw, k_col // kw, v_col // kw

    def prev_map(col):
        return lambda b, n: (b * nb + jnp.maximum(n - 1, 0), col)

    def cur_map(col):
        return lambda b, n: (b * nb + n, col)

    def next_map(col):
        return lambda b, n: (b * nb + jnp.minimum(n + 1, nb - 1), col)

    return pl.pallas_call(
        functools.partial(_attn_a_kernel, nb=nb, n_kv=n_kv, group=n_q // n_kv),
        grid=(bsz, nb),
        in_specs=[
            pl.BlockSpec(memory_space=pltpu.SMEM),
            pl.BlockSpec((blk, qw), cur_map(qb)),
            pl.BlockSpec((blk, kw), prev_map(kb)),
            pl.BlockSpec((blk, kw), cur_map(kb)),
            pl.BlockSpec((blk, kw), next_map(kb)),
            pl.BlockSpec((blk, kw), prev_map(vb)),
            pl.BlockSpec((blk, kw), cur_map(vb)),
            pl.BlockSpec((blk, kw), next_map(vb)),
            pl.BlockSpec((1, qw), lambda b, n: (0, 0)),
        ],
        out_specs=pl.BlockSpec((blk, qw), lambda b, n: (b * nb + n, 0)),
        out_shape=jax.ShapeDtypeStruct((m, qw), BF16),
        scratch_shapes=[pltpu.VMEM((blk, qw), F32)],
        compiler_params=pltpu.CompilerParams(dimension_semantics=("parallel", "arbitrary")),
        name="attn_a",
    )(sink, proj, proj, proj, proj, proj, proj, proj, gain)


def _bias_tables(rel_bias):
    n_h, n_dr, _ = rel_bias.shape
    col = jnp.arange(GRID_W)
    col_start = jnp.clip(col - NA_KW // 2, 0, GRID_W - NA_KW)
    inside = (col[None, :] >= col_start[:, None]) & (col[None, :] < col_start[:, None] + NA_KW)
    dc = jnp.clip(col[None, :] - col[:, None] + (NA_KW - 1), 0, 2 * NA_KW - 2)
    tc = jnp.where(inside[None, None], rel_bias.astype(F32)[:, :, dc], NEG)
    pairs = jnp.concatenate([tc[:, :-1], tc[:, 1:]], axis=-1)
    return pairs.reshape(n_h * (n_dr - 1), GRID_W, 2 * GRID_W) * LOG2E


def _attn_b_kernel(q_ref, k_ref, v_ref, bias_ref, gain_ref, o_ref, orow_ref, s_ref, p_ref, *, rows, n_h, rpb):
    rb = pl.program_id(1)
    band0 = jnp.clip(rpb * rb - NA_KH // 2, 0, rows - 2 * rpb)
    n_pair = 2 * NA_KH - 2
    gain = gain_ref[...]

    def row_body(ri, carry):
        r = rpb * rb + ri
        r0 = jnp.clip(r - NA_KH // 2, 0, rows - NA_KH)
        off = pl.multiple_of((r0 - band0) * GRID_W, GRID_W)
        qoff = pl.multiple_of(ri * GRID_W, GRID_W)
        dr0 = r0 - r + (NA_KH - 1)
        for h in range(n_h):
            hs = slice(h * HEAD_DIM, (h + 1) * HEAD_DIM)
            q = q_ref[pl.ds(qoff, GRID_W), hs]
            kk = k_ref[pl.ds(off, NA_KH * GRID_W), hs]
            bias = jnp.concatenate([bias_ref[h * n_pair + dr0 + 2 * t] for t in range(NA_KH // 2)], axis=1)
            s_ref[h * GRID_W:(h + 1) * GRID_W, :] = _dot_nt(q, kk) * (SCALE * LOG2E) + bias
        s = s_ref[...]
        mx = jnp.max(s, axis=-1, keepdims=True)
        p = jnp.exp2(s - mx)
        den = jnp.sum(p, axis=-1, keepdims=True)
        p_ref[...] = p.astype(BF16)
        for h in range(n_h):
            hs = slice(h * HEAD_DIM, (h + 1) * HEAD_DIM)
            rs = slice(h * GRID_W, (h + 1) * GRID_W)
            vv = v_ref[pl.ds(off, NA_KH * GRID_W), hs]
            orow_ref[:, hs] = _dot(p_ref[rs, :], vv) / den[rs]
        o_ref[pl.ds(qoff, GRID_W), :] = _rms(orow_ref[...], gain).astype(o_ref.dtype)
        return carry

    lax.fori_loop(0, rpb, row_body, 0)


def _attn_b(proj, bias_t, gain, *, bsz, s_len, q_col, k_col, v_col, n_h, rpb=8):
    m = proj.shape[0]
    rows = s_len // GRID_W
    width = n_h * HEAD_DIM
    assert rows % rpb == 0 and rows >= 2 * rpb and rpb >= NA_KH
    assert all(c % V7X_LANES == 0 for c in (q_col, k_col, v_col))
    nrb = rows // rpb
    band = 2 * rpb * GRID_W
    tq = rpb * GRID_W

    def band_map(col):
        def index(b, rb):
            start = jnp.clip(rpb * rb - NA_KH // 2, 0, rows - 2 * rpb)
            return (pl.multiple_of((b * rows + start) * GRID_W, GRID_W), col)
        return index

    keys = NA_KH * GRID_W
    vmem = (2 * (tq * width * 2 + 2 * band * width * 2 + tq * width * 2) + 2 * bias_t.size * 4 + GRID_W * width * 4
            + n_h * GRID_W * keys * 6)
    return pl.pallas_call(
        functools.partial(_attn_b_kernel, rows=rows, n_h=n_h, rpb=rpb),
        grid=(bsz, nrb),
        in_specs=[
            pl.BlockSpec((pl.Element(tq), pl.Element(width)),
                         lambda b, rb: (pl.multiple_of((b * nrb + rb) * tq, tq), q_col)),
            pl.BlockSpec((pl.Element(band), pl.Element(width)), band_map(k_col)),
            pl.BlockSpec((pl.Element(band), pl.Element(width)), band_map(v_col)),
            pl.BlockSpec(bias_t.shape, lambda b, rb: (0, 0, 0)),
            pl.BlockSpec((1, width), lambda b, rb: (0, 0)),
        ],
        out_specs=pl.BlockSpec((tq, width), lambda b, rb: (b * nrb + rb, 0)),
        out_shape=jax.ShapeDtypeStruct((m, width), BF16),
        scratch_shapes=[pltpu.VMEM((GRID_W, width), F32), pltpu.VMEM((n_h * GRID_W, keys), F32),
                        pltpu.VMEM((n_h * GRID_W, keys), BF16)],
        compiler_params=pltpu.CompilerParams(
            dimension_semantics=("parallel", "arbitrary"), vmem_limit_bytes=_vmem_limit(vmem)),
        name="attn_b",
    )(proj, proj, proj, bias_t, gain)


def _out_proj_kernel(a_ref, b_ref, wa_ref, wb_ref, x_ref, o_ref):
    o_ref[...] = x_ref[...] + (_dot(a_ref[...], wa_ref[...]) + _dot(b_ref[...], wb_ref[...]))


def _out_proj(oa, ob, w, x, *, tm=1024, tn=512):
    m, d = x.shape
    ka, kb = oa.shape[1], ob.shape[1]
    tm = min(tm, m)
    assert ka == kb and m % tm == 0 and d % tn == 0
    vmem = 2 * (2 * tm * ka * 2 + 2 * ka * tn * 2 + 2 * tm * tn * 4)
    return pl.pallas_call(
        _out_proj_kernel,
        grid=(m // tm, d // tn),
        in_specs=[
            pl.BlockSpec((tm, ka), lambda i, j: (i, 0)),
            pl.BlockSpec((tm, kb), lambda i, j: (i, 0)),
            pl.BlockSpec((ka, tn), lambda i, j: (0, j)),
            pl.BlockSpec((kb, tn), lambda i, j: (1, j)),
            pl.BlockSpec((tm, tn), lambda i, j: (i, j)),
        ],
        out_specs=pl.BlockSpec((tm, tn), lambda i, j: (i, j)),
        out_shape=jax.ShapeDtypeStruct((m, d), F32),
        compiler_params=pltpu.CompilerParams(
            dimension_semantics=("parallel", "arbitrary"), vmem_limit_bytes=_vmem_limit(vmem)),
        name="out_proj",
    )(oa, ob, w, w, x)


def _mem_kv_kernel(mem_ref, gain_ref, w_ref, o_ref, h_ref, *, tm):
    _norm_rows(mem_ref, gain_ref, h_ref, tm)
    o_ref[...] = _dot(h_ref[...], w_ref[...]).astype(o_ref.dtype)


def _mem_kv(mem, gain, w, *, tm=256):
    m, d = mem.shape
    n = w.shape[1]
    tm = min(tm, m)
    assert m % tm == 0
    vmem = 2 * tm * d * 4 + tm * d * 2 + 2 * d * n * 2 + 2 * tm * n * 2
    return pl.pallas_call(
        functools.partial(_mem_kv_kernel, tm=tm),
        grid=(m // tm,),
        in_specs=[
            pl.BlockSpec((tm, d), lambda i: (i, 0)),
            pl.BlockSpec((1, d), lambda i: (0, 0)),
            pl.BlockSpec((d, n), lambda i: (0, 0)),
        ],
        out_specs=pl.BlockSpec((tm, n), lambda i: (i, 0)),
        out_shape=jax.ShapeDtypeStruct((m, n), BF16),
        scratch_shapes=[pltpu.VMEM((tm, d), BF16)],
        compiler_params=pltpu.CompilerParams(
            dimension_semantics=("parallel",), vmem_limit_bytes=_vmem_limit(vmem)),
        name="mem_kv",
    )(mem, gain, w)


def _cross_kernel(x_ref, gain_ref, wq_ref, k_ref, v_ref, wo_ref, o_ref, h_ref, att_ref, *, tm):
    _norm_rows(x_ref, gain_ref, h_ref, tm)
    q = _dot(h_ref[...], wq_ref[...]).astype(BF16)
    for h in range(CA_HEADS):
        hs = slice(h * HEAD_DIM, (h + 1) * HEAD_DIM)
        s = _dot_nt(q[:, hs], k_ref[:, hs]) * (SCALE * LOG2E)
        mx = jnp.max(s, axis=-1, keepdims=True)
        p = jnp.exp2(s - mx)
        den = jnp.sum(p, axis=-1, keepdims=True)
        att_ref[:, hs] = (_dot(p.astype(BF16), v_ref[:, hs]) / den).astype(BF16)
    o_ref[...] = x_ref[...] + _dot(att_ref[...], wo_ref[...])


def _cross(x, gain, wq, kv, wo, *, s_len, n_mem, tm=512):
    m, d = x.shape
    caw = wq.shape[1]
    tm = min(tm, s_len)
    assert m % tm == 0 and s_len % tm == 0 and caw == CA_HEADS * HEAD_DIM
    per_seq = s_len // tm
    vmem = 4 * tm * d * 4 + tm * d * 2 + 2 * 2 * d * caw * 2 + 4 * n_mem * caw * 2 + tm * caw * 2
    return pl.pallas_call(
        functools.partial(_cross_kernel, tm=tm),
        grid=(m // tm,),
        in_specs=[
            pl.BlockSpec((tm, d), lambda i: (i, 0)),
            pl.BlockSpec((1, d), lambda i: (0, 0)),
            pl.BlockSpec((d, caw), lambda i: (0, 0)),
            pl.BlockSpec((n_mem, caw), lambda i: (i // per_seq, 0)),
            pl.BlockSpec((n_mem, caw), lambda i: (i // per_seq, 1)),
            pl.BlockSpec((caw, d), lambda i: (0, 0)),
        ],
        out_specs=pl.BlockSpec((tm, d), lambda i: (i, 0)),
        out_shape=jax.ShapeDtypeStruct((m, d), F32),
        scratch_shapes=[pltpu.VMEM((tm, d), BF16), pltpu.VMEM((tm, caw), BF16)],
        compiler_params=pltpu.CompilerParams(
            dimension_semantics=("parallel",), vmem_limit_bytes=_vmem_limit(vmem)),
        name="cross",
    )(x, gain, wq, kv, kv, wo)


def _trunk(x3, mem3, p):
    bsz, s_len, d = x3.shape
    n_mem = mem3.shape[1]
    x = x3.reshape(bsz * s_len, d)
    mem = mem3.reshape(bsz * n_mem, d)
    cos_t, sin_t = _rope_tables(s_len)

    x = _ffn(x, p["ffn1_norm"], p["ffn1_w_gate"], p["ffn1_w_up"], p["ffn1_w_down"], p["final_norm"], final=False)
    proj = _proj_in(x, p["mix_norm"], p["w_in"], cos_t, sin_t, s_len=s_len, rope_ranges=p["rope_ranges"])
    c = p["cols"]
    oa = _attn_a(proj, p["a_sink"], p["a_out_norm"], bsz=bsz, s_len=s_len,
                 q_col=c["qa"], k_col=c["ka"], v_col=c["va"], n_q=p["a_q_heads"], n_kv=p["a_kv_heads"])
    ob = _attn_b(proj, p["b_bias"], p["b_out_norm"], bsz=bsz, s_len=s_len,
                 q_col=c["qb"], k_col=c["kb"], v_col=c["vb"], n_h=p["b_heads"])
    x = _out_proj(oa, ob, p["w_out"], x)
    kv = _mem_kv(mem, p["mem_norm"], p["ca_w_kv"])
    x = _cross(x, p["ca_norm"], p["ca_w_q"], kv, p["ca_w_o"], s_len=s_len, n_mem=n_mem)
    y = _ffn(x, p["ffn2_norm"], p["ffn2_w_gate"], p["ffn2_w_up"], p["ffn2_w_down"], p["final_norm"], final=True)
    return y.reshape(bsz, s_len, d)


def _prepare(ffn1_norm, ffn1_w_gate, ffn1_w_up, ffn1_w_down, mix_norm, w_in, a_sink, b_rel_bias, a_out_norm,
             b_out_norm, w_out, ca_norm, mem_norm, ca_w_q, ca_w_kv, ca_w_o, ffn2_norm, ffn2_w_gate, ffn2_w_up,
             ffn2_w_down, final_norm):
    d = w_in.shape[1]
    n_heads = d // HEAD_DIM
    a_q = n_heads // 2
    a_kv = a_q // 4
    b_h = n_heads - a_q
    aq_w, akv_w, b_w = a_q * HEAD_DIM, a_kv * HEAD_DIM, b_h * HEAD_DIM
    cols = {"qa": 0, "ka": aq_w, "va": aq_w + akv_w, "qb": aq_w + 2 * akv_w, "kb": aq_w + 2 * akv_w + b_w,
            "vb": aq_w + 2 * akv_w + 2 * b_w}

    def gain(g):
        return g.reshape(1, -1).astype(F32)

    def wt(a):
        return a[0].astype(BF16)

    return {
        "ffn1_norm": gain(ffn1_norm[0]), "ffn1_w_gate": wt(ffn1_w_gate), "ffn1_w_up": wt(ffn1_w_up),
        "ffn1_w_down": wt(ffn1_w_down),
        "mix_norm": gain(mix_norm[0]), "w_in": wt(w_in), "cols": cols,
        "rope_ranges": ((cols["qa"], cols["qa"] + aq_w), (cols["ka"], cols["ka"] + akv_w)),
        "a_sink": a_sink[0].astype(F32), "b_bias": _bias_tables(b_rel_bias[0]),
        "a_out_norm": gain(a_out_norm[0]), "b_out_norm": gain(b_out_norm[0]), "w_out": wt(w_out),
        "a_q_heads": a_q, "a_kv_heads": a_kv, "b_heads": b_h,
        "ca_norm": gain(ca_norm[0]), "mem_norm": gain(mem_norm[0]), "ca_w_q": wt(ca_w_q), "ca_w_kv": wt(ca_w_kv),
        "ca_w_o": wt(ca_w_o),
        "ffn2_norm": gain(ffn2_norm[0]), "ffn2_w_gate": wt(ffn2_w_gate), "ffn2_w_up": wt(ffn2_w_up),
        "ffn2_w_down": wt(ffn2_w_down), "final_norm": gain(final_norm),
    }


def kernel(x_prompt, x_sample, mem_prompt, mem_sample, ffn1_norm, ffn1_w_gate, ffn1_w_up, ffn1_w_down, mix_norm, w_in, a_sink, b_rel_bias, a_out_norm, b_out_norm, w_out, ca_norm, mem_norm, ca_w_q, ca_w_kv, ca_w_o, ffn2_norm, ffn2_w_gate, ffn2_w_up, ffn2_w_down, final_norm):
    assert ffn1_w_gate.shape[0] == 1, "single-layer trunk"
    p = _prepare(ffn1_norm, ffn1_w_gate, ffn1_w_up, ffn1_w_down, mix_norm, w_in, a_sink, b_rel_bias, a_out_norm,
                 b_out_norm, w_out, ca_norm, mem_norm, ca_w_q, ca_w_kv, ca_w_o, ffn2_norm, ffn2_w_gate, ffn2_w_up,
                 ffn2_w_down, final_norm)
    return (_trunk(x_prompt, mem_prompt, p), _trunk(x_sample, mem_sample, p))
```

```python
import functools

import jax
import jax.numpy as jnp
from jax import lax
from jax.experimental import pallas as pl
from jax.experimental.pallas import tpu as pltpu

F32 = jnp.float32
BF16 = jnp.bfloat16

HEAD_DIM = 128
WINDOW = 128
ROPE_THETA = 500000.0
ROPE_DIM = HEAD_DIM // 4
GRID_W = 64
NA_KH = 8
NA_KW = 16
CA_HEADS = 4
EPS = 1e-6
NEG = -1e30
SCALE = HEAD_DIM ** -0.5
LOG2E = 1.4426950408889634

V7X_VMEM_BYTES = 64 * 1024 * 1024
V7X_LANES = 128
ROPE_ROW_SPLIT = 4
CROSS_ROW_SPLIT = 2
NORM_ROWS = 16
NORM_UNROLL = 4


def _vmem_limit(nbytes):
    return int(min(nbytes + 12 * 1024 * 1024, V7X_VMEM_BYTES - 4 * 1024 * 1024))


def _rms(x, gain):
    ms = jnp.mean(x * x, axis=-1, keepdims=True)
    return x * lax.rsqrt(ms + EPS) * gain


def _norm_rows(src_ref, gain_ref, dst_ref, rows):
    gain = gain_ref[...]

    def body(i, carry):
        r = pl.multiple_of(i * NORM_ROWS, NORM_ROWS)
        x = src_ref[pl.ds(r, NORM_ROWS), :]
        dst_ref[pl.ds(r, NORM_ROWS), :] = _rms(x, gain).astype(dst_ref.dtype)
        return carry

    lax.fori_loop(0, rows // NORM_ROWS, body, 0, unroll=NORM_UNROLL)


def _dot(a, b):
    return jnp.dot(a, b, preferred_element_type=F32)


def _dot_nt(a, b):
    return lax.dot_general(a, b, (((1,), (1,)), ((), ())), preferred_element_type=F32)


def _ffn_kernel(x_hbm, gin_ref, *refs, tm, nf, cps, ncol, final):
    w_refs, (gfin_ref, o_ref, xbuf, h_ref, rs_ref, sem) = refs[:3 * cps], refs[3 * cps:]
    i = pl.program_id(0)
    s = pl.program_id(1)
    n_tiles = pl.num_programs(0)
    n_steps = pl.num_programs(1)
    d = o_ref.shape[1]

    def x_copy(tile):
        return pltpu.make_async_copy(x_hbm.at[pl.ds(pl.multiple_of(tile * tm, tm), tm), :], xbuf, sem)

    @pl.when(s == 0)
    def _():
        @pl.when(i == 0)
        def _():
            x_copy(0).start()

        x_copy(i).wait()
        gin = gin_ref[...]

        def first(j, carry):
            rows = pl.ds(pl.multiple_of(j * NORM_ROWS, NORM_ROWS), NORM_ROWS)
            x = xbuf[rows, :]
            h_ref[rows, :] = _rms(x, gin).astype(h_ref.dtype)
            o_ref[rows, :] = x
            return carry

        lax.fori_loop(0, tm // NORM_ROWS, first, 0, unroll=NORM_UNROLL)

        @pl.when(i + 1 < n_tiles)
        def _():
            x_copy(i + 1).start()

    def chunk(wg_ref, wu_ref, wd_ref):
        h = h_ref[...]
        g = _dot(h, wg_ref[...])
        u = _dot(h, wu_ref[...])
        act = (0.5 * (g / (1.0 + jnp.exp(-g)) * u)).astype(BF16)
        for c in range(0, d, ncol):
            o_ref[:, c:c + ncol] += _dot(act, wd_ref[:, c:c + ncol])

    def chunks(count):
        for c in range(count):
            chunk(*w_refs[3 * c:3 * c + 3])

    tail = nf - (pl.cdiv(nf, cps) - 1) * cps
    if tail == cps:
        chunks(cps)
    else:
        pl.when(s < n_steps - 1)(functools.partial(chunks, cps))
        pl.when(s == n_steps - 1)(functools.partial(chunks, tail))

    if final:
        @pl.when(s == n_steps - 1)
        def _():
            def row_scale(j, carry):
                rows = pl.ds(pl.multiple_of(j * NORM_ROWS, NORM_ROWS), NORM_ROWS)
                y = o_ref[rows, :]
                ms = jnp.mean(y * y, axis=-1, keepdims=True)
                rs_ref[rows, :] = jnp.broadcast_to(lax.rsqrt(ms + EPS), (NORM_ROWS, rs_ref.shape[1]))
                return carry

            lax.fori_loop(0, tm // NORM_ROWS, row_scale, 0, unroll=NORM_UNROLL)
            gfin = gfin_ref[...]

            def scale(j, carry):
                rows = pl.ds(pl.multiple_of(j * NORM_ROWS, NORM_ROWS), NORM_ROWS)
                o_ref[rows, :] = o_ref[rows, :] * rs_ref[rows, :][:, :1] * gfin
                return carry

            lax.fori_loop(0, tm // NORM_ROWS, scale, 0, unroll=NORM_UNROLL)


def _ffn(x, gin, wg, wu, wd, gfin, *, final, tm=512, tf=256, cps=2, ncol=512):
    m, d = x.shape
    dff = wg.shape[1]
    tm = min(tm, m)
    nf = dff // tf
    cps = min(cps, nf)
    n_steps = pl.cdiv(nf, cps)
    assert m % tm == 0 and dff % tf == 0 and d % ncol == 0

    def chunk_specs(c):
        def col(i, s):
            return (0, jnp.minimum(s * cps + c, nf - 1))

        def row(i, s):
            return (jnp.minimum(s * cps + c, nf - 1), 0)

        return [pl.BlockSpec((d, tf), col), pl.BlockSpec((d, tf), col), pl.BlockSpec((tf, d), row)]

    w_specs = [spec for c in range(cps) for spec in chunk_specs(c)]
    vmem = tm * d * 4 + 2 * tm * d * 4 + tm * d * 2 + 2 * cps * 3 * d * tf * 2
    return pl.pallas_call(
        functools.partial(_ffn_kernel, tm=tm, nf=nf, cps=cps, ncol=ncol, final=final),
        grid=(m // tm, n_steps),
        in_specs=[
            pl.BlockSpec(memory_space=pl.ANY),
            pl.BlockSpec((1, d), lambda i, s: (0, 0)),
            *w_specs,
            pl.BlockSpec((1, d), lambda i, s: (0, 0)),
        ],
        out_specs=pl.BlockSpec((tm, d), lambda i, s: (i, 0)),
        out_shape=jax.ShapeDtypeStruct((m, d), F32),
        scratch_shapes=[pltpu.VMEM((tm, d), F32), pltpu.VMEM((tm, d), BF16), pltpu.VMEM((tm, V7X_LANES), F32),
                        pltpu.SemaphoreType.DMA],
        compiler_params=pltpu.CompilerParams(
            dimension_semantics=("arbitrary", "arbitrary"), vmem_limit_bytes=_vmem_limit(vmem)),
        name="ffn_final" if final else "ffn",
    )(x, gin, *([wg, wu, wd] * cps), gfin)


def _rope_tables(s_len):
    half = ROPE_DIM // 2
    inv = 1.0 / (ROPE_THETA ** (jnp.arange(half, dtype=F32) / half))
    ang = jnp.arange(s_len).astype(F32)[:, None] * inv[None, :]
    cos, sin = jnp.cos(ang), jnp.sin(ang)
    rest = HEAD_DIM - ROPE_DIM
    cos_t = jnp.concatenate([cos, cos, jnp.ones((s_len, rest), F32)], axis=1)
    sin_t = jnp.concatenate([-sin, sin, jnp.zeros((s_len, rest), F32)], axis=1)
    return cos_t, sin_t


def _proj_kernel(x_ref, gin_ref, w_ref, cos_ref, sin_ref, o_ref, h_ref, *, tm, tn, rope_ranges):
    j = pl.program_id(1)
    col = j * tn
    is_rope = functools.reduce(jnp.logical_or, [(col >= lo) & (col < hi) for lo, hi in rope_ranges])

    def rope_step(with_norm):
        gain = gin_ref[...]
        cos = cos_ref[...]
        sin = sin_ref[...]
        half = ROPE_DIM // 2
        rt = tm // ROPE_ROW_SPLIT
        lane = lax.broadcasted_iota(jnp.int32, (rt, HEAD_DIM), 1)
        for r0 in range(0, tm, rt):
            if with_norm:
                for g0 in range(r0, r0 + rt, NORM_ROWS):
                    h_ref[g0:g0 + NORM_ROWS, :] = _rms(x_ref[g0:g0 + NORM_ROWS, :], gain).astype(h_ref.dtype)
            acc = _dot(h_ref[r0:r0 + rt, :], w_ref[...])
            for c in range(0, tn, HEAD_DIM):
                xh = acc[:, c:c + HEAD_DIM]
                partner = jnp.where(lane < half, pltpu.roll(xh, HEAD_DIM - half, 1), pltpu.roll(xh, half, 1))
                rot = xh * cos[r0:r0 + rt] + partner * sin[r0:r0 + rt]
                o_ref[r0:r0 + rt, c:c + HEAD_DIM] = jnp.where(lane < ROPE_DIM, rot, xh).astype(o_ref.dtype)

    pl.when(j == 0)(functools.partial(rope_step, True))
    pl.when(jnp.logical_and(is_rope, j > 0))(functools.partial(rope_step, False))

    @pl.when(jnp.logical_not(is_rope))
    def _():
        o_ref[...] = _dot(h_ref[...], w_ref[...]).astype(o_ref.dtype)


def _proj_in(x, gin, w, cos_t, sin_t, *, s_len, rope_ranges, tm=1024, tn=512):
    m, d = x.shape
    n = w.shape[1]
    tm = min(tm, s_len)
    assert m % tm == 0 and s_len % tm == 0 and n % tn == 0
    assert all(lo % tn == 0 and hi % tn == 0 for lo, hi in rope_ranges) and rope_ranges[0][0] == 0
    per_seq = s_len // tm
    vmem = 2 * tm * d * 4 + tm * d * 2 + 2 * d * tn * 2 + 2 * tm * tn * 2 + 4 * tm * HEAD_DIM * 4
    return pl.pallas_call(
        functools.partial(_proj_kernel, tm=tm, tn=tn, rope_ranges=rope_ranges),
        grid=(m // tm, n // tn),
        in_specs=[
            pl.BlockSpec((tm, d), lambda i, j: (i, 0)),
            pl.BlockSpec((1, d), lambda i, j: (0, 0)),
            pl.BlockSpec((d, tn), lambda i, j: (0, j)),
            pl.BlockSpec((tm, HEAD_DIM), lambda i, j: (i % per_seq, 0)),
            pl.BlockSpec((tm, HEAD_DIM), lambda i, j: (i % per_seq, 0)),
        ],
        out_specs=pl.BlockSpec((tm, tn), lambda i, j: (i, j)),
        out_shape=jax.ShapeDtypeStruct((m, n), BF16),
        scratch_shapes=[pltpu.VMEM((tm, d), BF16)],
        compiler_params=pltpu.CompilerParams(
            dimension_semantics=("parallel", "arbitrary"), vmem_limit_bytes=_vmem_limit(vmem)),
        name="proj_in",
    )(x, gin, w, cos_t, sin_t)


def _attn_a_kernel(sink_ref, q_ref, kp_ref, kc_ref, kn_ref, vp_ref, vc_ref, vn_ref, gain_ref, o_ref, oacc_ref,
                   *, nb, n_kv, group):
    n = pl.program_id(1)
    blk = WINDOW
    rows = group * blk
    rr = lax.broadcasted_iota(jnp.int32, (rows, 3 * blk), 0) & (blk - 1)
    cc = lax.broadcasted_iota(jnp.int32, (rows, 3 * blk), 1)
    lo = jnp.where(n == 0, blk, 0)
    hi = jnp.where(n == nb - 1, 2 * blk, 3 * blk)
    valid = (cc >= rr) & (cc <= rr + 2 * WINDOW) & (cc >= lo) & (cc < hi)
    grp = lax.broadcasted_iota(jnp.int32, (rows, 1), 0) // blk
    for kv in range(n_kv):
        ks = slice(kv * HEAD_DIM, (kv + 1) * HEAD_DIM)
        k = jnp.concatenate([kp_ref[:, ks], kc_ref[:, ks], kn_ref[:, ks]], axis=0)
        v = jnp.concatenate([vp_ref[:, ks], vc_ref[:, ks], vn_ref[:, ks]], axis=0)
        q = jnp.concatenate(
            [q_ref[:, (kv * group + g) * HEAD_DIM:(kv * group + g + 1) * HEAD_DIM] for g in range(group)], axis=0)
        s = _dot_nt(q, k) * (SCALE * LOG2E)
        s = jnp.where(valid, s, NEG)
        sink = jnp.zeros((rows, 1), F32)
        for g in range(group):
            sink = jnp.where(grp == g, sink_ref[kv * group + g] * LOG2E, sink)
        mx = jnp.maximum(jnp.max(s, axis=-1, keepdims=True), sink)
        p = jnp.exp2(s - mx)
        den = jnp.sum(p, axis=-1, keepdims=True) + jnp.exp2(sink - mx)
        o = _dot(p.astype(BF16), v) / den
        for g in range(group):
            h0 = (kv * group + g) * HEAD_DIM
            oacc_ref[:, h0:h0 + HEAD_DIM] = o[g * blk:(g + 1) * blk, :]
    o_ref[...] = _rms(oacc_ref[...], gain_ref[...]).astype(o_ref.dtype)


def _attn_a(proj, sink, gain, *, bsz, s_len, q_col, k_col, v_col, n_q, n_kv):
    m = proj.shape[0]
    blk = WINDOW
    nb = s_len // blk
    qw, kw = n_q * HEAD_DIM, n_kv * HEAD_DIM
    assert s_len % blk == 0 and q_col % qw == 0 and k_col % kw == 0 and v_col % kw == 0
    qb, kb, vb = q_col // qw, k_col // kw, v_col // kw

    def prev_map(col):
        return lambda b, n: (b * nb + jnp.maximum(n - 1, 0), col)

    def cur_map(col):
        return lambda b, n: (b * nb + n, col)

    def next_map(col):
        return lambda b, n: (b * nb + jnp.minimum(n + 1, nb - 1), col)

    return pl.pallas_call(
        functools.partial(_attn_a_kernel, nb=nb, n_kv=n_kv, group=n_q // n_kv),
        grid=(bsz, nb),
        in_specs=[
            pl.BlockSpec(memory_space=pltpu.SMEM),
            pl.BlockSpec((blk, qw), cur_map(qb)),
            pl.BlockSpec((blk, kw), prev_map(kb)),
            pl.BlockSpec((blk, kw), cur_map(kb)),
            pl.BlockSpec((blk, kw), next_map(kb)),
            pl.BlockSpec((blk, kw), prev_map(vb)),
            pl.BlockSpec((blk, kw), cur_map(vb)),
            pl.BlockSpec((blk, kw), next_map(vb)),
            pl.BlockSpec((1, qw), lambda b, n: (0, 0)),
        ],
        out_specs=pl.BlockSpec((blk, qw), lambda b, n: (b * nb + n, 0)),
        out_shape=jax.ShapeDtypeStruct((m, qw), BF16),
        scratch_shapes=[pltpu.VMEM((blk, qw), F32)],
        compiler_params=pltpu.CompilerParams(dimension_semantics=("parallel", "arbitrary")),
        name="attn_a",
    )(sink, proj, proj, proj, proj, proj, proj, proj, gain)


def _bias_tables(rel_bias):
    n_h, n_dr, _ = rel_bias.shape
    col = jnp.arange(GRID_W)
    col_start = jnp.clip(col - NA_KW // 2, 0, GRID_W - NA_KW)
    inside = (col[None, :] >= col_start[:, None]) & (col[None, :] < col_start[:, None] + NA_KW)
    dc = jnp.clip(col[None, :] - col[:, None] + (NA_KW - 1), 0, 2 * NA_KW - 2)
    tc = jnp.where(inside[None, None], rel_bias.astype(F32)[:, :, dc], NEG)
    pairs = jnp.concatenate([tc[:, :-1], tc[:, 1:]], axis=-1)
    return pairs.reshape(n_h * (n_dr - 1), GRID_W, 2 * GRID_W) * LOG2E


def _attn_b_kernel(q_ref, k_ref, v_ref, bias_ref, gain_ref, o_ref, orow_ref, s_ref, p_ref, *, rows, n_h, rpb):
    rb = pl.program_id(1)
    band0 = jnp.clip(rpb * rb - NA_KH // 2, 0, rows - 2 * rpb)
    n_pair = 2 * NA_KH - 2
    gain = gain_ref[...]

    def row_body(ri, carry):
        r = rpb * rb + ri
        r0 = jnp.clip(r - NA_KH // 2, 0, rows - NA_KH)
        off = pl.multiple_of((r0 - band0) * GRID_W, GRID_W)
        qoff = pl.multiple_of(ri * GRID_W, GRID_W)
        dr0 = r0 - r + (NA_KH - 1)
        for h in range(n_h):
            hs = slice(h * HEAD_DIM, (h + 1) * HEAD_DIM)
            q = q_ref[pl.ds(qoff, GRID_W), hs]
            kk = k_ref[pl.ds(off, NA_KH * GRID_W), hs]
            bias = jnp.concatenate([bias_ref[h * n_pair + dr0 + 2 * t] for t in range(NA_KH // 2)], axis=1)
            s_ref[h * GRID_W:(h + 1) * GRID_W, :] = _dot_nt(q, kk) * (SCALE * LOG2E) + bias
        s = s_ref[...]
        mx = jnp.max(s, axis=-1, keepdims=True)
        p = jnp.exp2(s - mx)
        den = jnp.sum(p, axis=-1, keepdims=True)
        p_ref[...] = p.astype(BF16)
        for h in range(n_h):
            hs = slice(h * HEAD_DIM, (h + 1) * HEAD_DIM)
            rs = slice(h * GRID_W, (h + 1) * GRID_W)
            vv = v_ref[pl.ds(off, NA_KH * GRID_W), hs]
            orow_ref[:, hs] = _dot(p_ref[rs, :], vv) / den[rs]
        o_ref[pl.ds(qoff, GRID_W), :] = _rms(orow_ref[...], gain).astype(o_ref.dtype)
        return carry

    lax.fori_loop(0, rpb, row_body, 0)


def _attn_b(proj, bias_t, gain, *, bsz, s_len, q_col, k_col, v_col, n_h, rpb=8):
    m = proj.shape[0]
    rows = s_len // GRID_W
    width = n_h * HEAD_DIM
    assert rows % rpb == 0 and rows >= 2 * rpb and rpb >= NA_KH
    assert all(c % V7X_LANES == 0 for c in (q_col, k_col, v_col))
    nrb = rows // rpb
    band = 2 * rpb * GRID_W
    tq = rpb * GRID_W

    def band_map(col):
        def index(b, rb):
            start = jnp.clip(rpb * rb - NA_KH // 2, 0, rows - 2 * rpb)
            return (pl.multiple_of((b * rows + start) * GRID_W, GRID_W), col)
        return index

    keys = NA_KH * GRID_W
    vmem = (2 * (tq * width * 2 + 2 * band * width * 2 + tq * width * 2) + 2 * bias_t.size * 4 + GRID_W * width * 4
            + n_h * GRID_W * keys * 6)
    return pl.pallas_call(
        functools.partial(_attn_b_kernel, rows=rows, n_h=n_h, rpb=rpb),
        grid=(bsz, nrb),
        in_specs=[
            pl.BlockSpec((pl.Element(tq), pl.Element(width)),
                         lambda b, rb: (pl.multiple_of((b * nrb + rb) * tq, tq), q_col)),
            pl.BlockSpec((pl.Element(band), pl.Element(width)), band_map(k_col)),
            pl.BlockSpec((pl.Element(band), pl.Element(width)), band_map(v_col)),
            pl.BlockSpec(bias_t.shape, lambda b, rb: (0, 0, 0)),
            pl.BlockSpec((1, width), lambda b, rb: (0, 0)),
        ],
        out_specs=pl.BlockSpec((tq, width), lambda b, rb: (b * nrb + rb, 0)),
        out_shape=jax.ShapeDtypeStruct((m, width), BF16),
        scratch_shapes=[pltpu.VMEM((GRID_W, width), F32), pltpu.VMEM((n_h * GRID_W, keys), F32),
                        pltpu.VMEM((n_h * GRID_W, keys), BF16)],
        compiler_params=pltpu.CompilerParams(
            dimension_semantics=("parallel", "arbitrary"), vmem_limit_bytes=_vmem_limit(vmem)),
        name="attn_b",
    )(proj, proj, proj, bias_t, gain)


def _out_proj_kernel(a_ref, b_ref, wa_ref, wb_ref, x_ref, o_ref):
    o_ref[...] = x_ref[...] + (_dot(a_ref[...], wa_ref[...]) + _dot(b_ref[...], wb_ref[...]))


def _out_proj(oa, ob, w, x, *, tm=1024, tn=512):
    m, d = x.shape
    ka, kb = oa.shape[1], ob.shape[1]
    tm = min(tm, m)
    assert ka == kb and m % tm == 0 and d % tn == 0
    vmem = 2 * (2 * tm * ka * 2 + 2 * ka * tn * 2 + 2 * tm * tn * 4)
    return pl.pallas_call(
        _out_proj_kernel,
        grid=(m // tm, d // tn),
        in_specs=[
            pl.BlockSpec((tm, ka), lambda i, j: (i, 0)),
            pl.BlockSpec((tm, kb), lambda i, j: (i, 0)),
            pl.BlockSpec((ka, tn), lambda i, j: (0, j)),
            pl.BlockSpec((kb, tn), lambda i, j: (1, j)),
            pl.BlockSpec((tm, tn), lambda i, j: (i, j)),
        ],
        out_specs=pl.BlockSpec((tm, tn), lambda i, j: (i, j)),
        out_shape=jax.ShapeDtypeStruct((m, d), F32),
        compiler_params=pltpu.CompilerParams(
            dimension_semantics=("parallel", "arbitrary"), vmem_limit_bytes=_vmem_limit(vmem)),
        name="out_proj",
    )(oa, ob, w, w, x)


def _mem_kv_kernel(mem_ref, gain_ref, w_ref, o_ref, h_ref, *, tm):
    _norm_rows(mem_ref, gain_ref, h_ref, tm)
    o_ref[...] = _dot(h_ref[...], w_ref[...]).astype(o_ref.dtype)


def _mem_kv(mem, gain, w, *, tm=256):
    m, d = mem.shape
    n = w.shape[1]
    tm = min(tm, m)
    assert m % tm == 0
    vmem = 2 * tm * d * 4 + tm * d * 2 + 2 * d * n * 2 + 2 * tm * n * 2
    return pl.pallas_call(
        functools.partial(_mem_kv_kernel, tm=tm),
        grid=(m // tm,),
        in_specs=[
            pl.BlockSpec((tm, d), lambda i: (i, 0)),
            pl.BlockSpec((1, d), lambda i: (0, 0)),
            pl.BlockSpec((d, n), lambda i: (0, 0)),
        ],
        out_specs=pl.BlockSpec((tm, n), lambda i: (i, 0)),
        out_shape=jax.ShapeDtypeStruct((m, n), BF16),
        scratch_shapes=[pltpu.VMEM((tm, d), BF16)],
        compiler_params=pltpu.CompilerParams(
            dimension_semantics=("parallel",), vmem_limit_bytes=_vmem_limit(vmem)),
        name="mem_kv",
    )(mem, gain, w)


def _cross_kernel(x_ref, gain_ref, wq_ref, k_ref, v_ref, wo_ref, o_ref, h_ref, att_ref, *, tm):
    gain = gain_ref[...]
    rt = tm // CROSS_ROW_SPLIT
    for r0 in range(0, tm, rt):
        for g0 in range(r0, r0 + rt, NORM_ROWS):
            h_ref[g0:g0 + NORM_ROWS, :] = _rms(x_ref[g0:g0 + NORM_ROWS, :], gain).astype(h_ref.dtype)
        q = _dot(h_ref[r0:r0 + rt, :], wq_ref[...]).astype(BF16)
        for h in range(CA_HEADS):
            hs = slice(h * HEAD_DIM, (h + 1) * HEAD_DIM)
            s = _dot_nt(q[:, hs], k_ref[:, hs]) * (SCALE * LOG2E)
            mx = jnp.max(s, axis=-1, keepdims=True)
            p = jnp.exp2(s - mx)
            den = jnp.sum(p, axis=-1, keepdims=True)
            att_ref[r0:r0 + rt, hs] = (_dot(p.astype(BF16), v_ref[:, hs]) / den).astype(BF16)
        o_ref[r0:r0 + rt, :] = x_ref[r0:r0 + rt, :] + _dot(att_ref[r0:r0 + rt, :], wo_ref[...])


def _cross(x, gain, wq, kv, wo, *, s_len, n_mem, tm=512):
    m, d = x.shape
    caw = wq.shape[1]
    tm = min(tm, s_len)
    assert m % tm == 0 and s_len % tm == 0 and caw == CA_HEADS * HEAD_DIM
    per_seq = s_len // tm
    vmem = 4 * tm * d * 4 + tm * d * 2 + 2 * 2 * d * caw * 2 + 4 * n_mem * caw * 2 + tm * caw * 2
    return pl.pallas_call(
        functools.partial(_cross_kernel, tm=tm),
        grid=(m // tm,),
        in_specs=[
            pl.BlockSpec((tm, d), lambda i: (i, 0)),
            pl.BlockSpec((1, d), lambda i: (0, 0)),
            pl.BlockSpec((d, caw), lambda i: (0, 0)),
            pl.BlockSpec((n_mem, caw), lambda i: (i // per_seq, 0)),
            pl.BlockSpec((n_mem, caw), lambda i: (i // per_seq, 1)),
            pl.BlockSpec((caw, d), lambda i: (0, 0)),
        ],
        out_specs=pl.BlockSpec((tm, d), lambda i: (i, 0)),
        out_shape=jax.ShapeDtypeStruct((m, d), F32),
        scratch_shapes=[pltpu.VMEM((tm, d), BF16), pltpu.VMEM((tm, caw), BF16)],
        compiler_params=pltpu.CompilerParams(
            dimension_semantics=("parallel",), vmem_limit_bytes=_vmem_limit(vmem)),
        name="cross",
    )(x, gain, wq, kv, kv, wo)


def _trunk(x3, mem3, p):
    bsz, s_len, d = x3.shape
    n_mem = mem3.shape[1]
    x = x3.reshape(bsz * s_len, d)
    mem = mem3.reshape(bsz * n_mem, d)
    cos_t, sin_t = _rope_tables(s_len)

    x = _ffn(x, p["ffn1_norm"], p["ffn1_w_gate"], p["ffn1_w_up"], p["ffn1_w_down"], p["final_norm"], final=False)
    proj = _proj_in(x, p["mix_norm"], p["w_in"], cos_t, sin_t, s_len=s_len, rope_ranges=p["rope_ranges"])
    c = p["cols"]
    oa = _attn_a(proj, p["a_sink"], p["a_out_norm"], bsz=bsz, s_len=s_len,
                 q_col=c["qa"], k_col=c["ka"], v_col=c["va"], n_q=p["a_q_heads"], n_kv=p["a_kv_heads"])
    ob = _attn_b(proj, p["b_bias"], p["b_out_norm"], bsz=bsz, s_len=s_len,
                 q_col=c["qb"], k_col=c["kb"], v_col=c["vb"], n_h=p["b_heads"])
    x = _out_proj(oa, ob, p["w_out"], x)
    kv = _mem_kv(mem, p["mem_norm"], p["ca_w_kv"])
    x = _cross(x, p["ca_norm"], p["ca_w_q"], kv, p["ca_w_o"], s_len=s_len, n_mem=n_mem)
    y = _ffn(x, p["ffn2_norm"], p["ffn2_w_gate"], p["ffn2_w_up"], p["ffn2_w_down"], p["final_norm"], final=True)
    return y.reshape(bsz, s_len, d)


def _prepare(ffn1_norm, ffn1_w_gate, ffn1_w_up, ffn1_w_down, mix_norm, w_in, a_sink, b_rel_bias, a_out_norm,
             b_out_norm, w_out, ca_norm, mem_norm, ca_w_q, ca_w_kv, ca_w_o, ffn2_norm, ffn2_w_gate, ffn2_w_up,
             ffn2_w_down, final_norm):
    d = w_in.shape[1]
    n_heads = d // HEAD_DIM
    a_q = n_heads // 2
    a_kv = a_q // 4
    b_h = n_heads - a_q
    aq_w, akv_w, b_w = a_q * HEAD_DIM, a_kv * HEAD_DIM, b_h * HEAD_DIM
    cols = {"qa": 0, "ka": aq_w, "va": aq_w + akv_w, "qb": aq_w + 2 * akv_w, "kb": aq_w + 2 * akv_w + b_w,
            "vb": aq_w + 2 * akv_w + 2 * b_w}

    def gain(g):
        return g.reshape(1, -1).astype(F32)

    def wt(a):
        return a[0].astype(BF16)

    return {
        "ffn1_norm": gain(ffn1_norm[0]), "ffn1_w_gate": wt(ffn1_w_gate), "ffn1_w_up": wt(ffn1_w_up),
        "ffn1_w_down": wt(ffn1_w_down),
        "mix_norm": gain(mix_norm[0]), "w_in": wt(w_in), "cols": cols,
        "rope_ranges": ((cols["qa"], cols["qa"] + aq_w), (cols["ka"], cols["ka"] + akv_w)),
        "a_sink": a_sink[0].astype(F32), "b_bias": _bias_tables(b_rel_bias[0]),
        "a_out_norm": gain(a_out_norm[0]), "b_out_norm": gain(b_out_norm[0]), "w_out": wt(w_out),
        "a_q_heads": a_q, "a_kv_heads": a_kv, "b_heads": b_h,
        "ca_norm": gain(ca_norm[0]), "mem_norm": gain(mem_norm[0]), "ca_w_q": wt(ca_w_q), "ca_w_kv": wt(ca_w_kv),
        "ca_w_o": wt(ca_w_o),
        "ffn2_norm": gain(ffn2_norm[0]), "ffn2_w_gate": wt(ffn2_w_gate), "ffn2_w_up": wt(ffn2_w_up),
        "ffn2_w_down": wt(ffn2_w_down), "final_norm": gain(final_norm),
    }


def kernel(x_prompt, x_sample, mem_prompt, mem_sample, ffn1_norm, ffn1_w_gate, ffn1_w_up, ffn1_w_down, mix_norm, w_in, a_sink, b_rel_bias, a_out_norm, b_out_norm, w_out, ca_norm, mem_norm, ca_w_q, ca_w_kv, ca_w_o, ffn2_norm, ffn2_w_gate, ffn2_w_up, ffn2_w_down, final_norm):
    assert ffn1_w_gate.shape[0] == 1, "single-layer trunk"
    p = _prepare(ffn1_norm, ffn1_w_gate, ffn1_w_up, ffn1_w_down, mix_norm, w_in, a_sink, b_rel_bias, a_out_norm,
                 b_out_norm, w_out, ca_norm, mem_norm, ca_w_q, ca_w_kv, ca_w_o, ffn2_norm, ffn2_w_gate, ffn2_w_up,
                 ffn2_w_down, final_norm)
    return (_trunk(x_prompt, mem_prompt, p), _trunk(x_sample, mem_sample, p))
```

```python
import functools

import jax
import jax.numpy as jnp
from jax import lax
from jax.experimental import pallas as pl
from jax.experimental.pallas import tpu as pltpu

F32 = jnp.float32
BF16 = jnp.bfloat16

HEAD_DIM = 128
WINDOW = 128
ROPE_THETA = 500000.0
ROPE_DIM = HEAD_DIM // 4
GRID_W = 64
NA_KH = 8
NA_KW = 16
CA_HEADS = 4
EPS = 1e-6
NEG = -1e30
SCALE = HEAD_DIM ** -0.5
LOG2E = 1.4426950408889634

V7X_VMEM_BYTES = 64 * 1024 * 1024
V7X_LANES = 128
ROPE_ROW_SPLIT = 4
CROSS_ROW_SPLIT = 2
CAST_ROWS = 16
NORM_ROWS = 16
NORM_UNROLL = 4


def _vmem_limit(nbytes):
    return int(min(nbytes + 12 * 1024 * 1024, V7X_VMEM_BYTES - 4 * 1024 * 1024))


def _rms(x, gain):
    ms = jnp.mean(x * x, axis=-1, keepdims=True)
    return x * lax.rsqrt(ms + EPS) * gain


def _norm_rows(src_ref, gain_ref, dst_ref, rows):
    gain = gain_ref[...]

    def body(i, carry):
        r = pl.multiple_of(i * NORM_ROWS, NORM_ROWS)
        x = src_ref[pl.ds(r, NORM_ROWS), :]
        dst_ref[pl.ds(r, NORM_ROWS), :] = _rms(x, gain).astype(dst_ref.dtype)
        return carry

    lax.fori_loop(0, rows // NORM_ROWS, body, 0, unroll=NORM_UNROLL)


def _dot(a, b):
    return jnp.dot(a, b, preferred_element_type=F32)


def _dot_nt(a, b):
    return lax.dot_general(a, b, (((1,), (1,)), ((), ())), preferred_element_type=F32)


def _ffn_kernel(x_hbm, gin_ref, *refs, tm, nf, cps, ncol, n_cast, final):
    w_refs, refs = refs[:3 * cps], refs[3 * cps:]
    gfin_ref, cast_in, refs = refs[0], refs[1:1 + n_cast], refs[1 + n_cast:]
    o_ref, cast_out, (xbuf, h_ref, rs_ref, sem) = refs[0], refs[1:1 + n_cast], refs[1 + n_cast:]
    i = pl.program_id(0)
    s = pl.program_id(1)
    n_tiles = pl.num_programs(0)
    n_steps = pl.num_programs(1)
    d = o_ref.shape[1]

    def x_copy(tile):
        return pltpu.make_async_copy(x_hbm.at[pl.ds(pl.multiple_of(tile * tm, tm), tm), :], xbuf, sem)

    @pl.when(s == 0)
    def _():
        @pl.when(i == 0)
        def _():
            x_copy(0).start()

        x_copy(i).wait()
        gin = gin_ref[...]

        def first(j, carry):
            rows = pl.ds(pl.multiple_of(j * NORM_ROWS, NORM_ROWS), NORM_ROWS)
            x = xbuf[rows, :]
            h_ref[rows, :] = _rms(x, gin).astype(h_ref.dtype)
            o_ref[rows, :] = x
            return carry

        lax.fori_loop(0, tm // NORM_ROWS, first, 0, unroll=NORM_UNROLL)

        @pl.when(i + 1 < n_tiles)
        def _():
            x_copy(i + 1).start()

    def chunk(wg_ref, wu_ref, wd_ref):
        h = h_ref[...]
        g = _dot(h, wg_ref[...])
        u = _dot(h, wu_ref[...])
        act = (0.5 * (g / (1.0 + jnp.exp(-g)) * u)).astype(BF16)
        for c in range(0, d, ncol):
            o_ref[:, c:c + ncol] += _dot(act, wd_ref[:, c:c + ncol])

    def chunks(count):
        for c in range(count):
            chunk(*w_refs[3 * c:3 * c + 3])
        for src, dst in zip(cast_in, cast_out):
            dst[...] = src[...].astype(dst.dtype)

    tail = nf - (pl.cdiv(nf, cps) - 1) * cps
    if tail == cps:
        chunks(cps)
    else:
        pl.when(s < n_steps - 1)(functools.partial(chunks, cps))
        pl.when(s == n_steps - 1)(functools.partial(chunks, tail))

    if final:
        @pl.when(s == n_steps - 1)
        def _():
            def row_scale(j, carry):
                rows = pl.ds(pl.multiple_of(j * NORM_ROWS, NORM_ROWS), NORM_ROWS)
                y = o_ref[rows, :]
                ms = jnp.mean(y * y, axis=-1, keepdims=True)
                rs_ref[rows, :] = jnp.broadcast_to(lax.rsqrt(ms + EPS), (NORM_ROWS, rs_ref.shape[1]))
                return carry

            lax.fori_loop(0, tm // NORM_ROWS, row_scale, 0, unroll=NORM_UNROLL)
            gfin = gfin_ref[...]

            def scale(j, carry):
                rows = pl.ds(pl.multiple_of(j * NORM_ROWS, NORM_ROWS), NORM_ROWS)
                o_ref[rows, :] = o_ref[rows, :] * rs_ref[rows, :][:, :1] * gfin
                return carry

            lax.fori_loop(0, tm // NORM_ROWS, scale, 0, unroll=NORM_UNROLL)


def _cast_spec(shape, n_tiles, n_steps):
    r, c = shape
    grid_steps = n_tiles * n_steps
    assert r % CAST_ROWS == 0
    groups = r // CAST_ROWS
    per_block = min(k for k in range(1, groups + 1) if groups % k == 0 and groups // k <= grid_steps)
    row_blocks = groups // per_block
    splits = 1
    while row_blocks * splits * 2 <= grid_steps and c % (splits * 2 * V7X_LANES) == 0:
        splits *= 2
    n_blocks = row_blocks * splits

    def index(i, s):
        t = jnp.minimum(i * n_steps + s, n_blocks - 1)
        return (t // splits, t % splits)

    return pl.BlockSpec((per_block * CAST_ROWS, c // splits), index)


def _ffn(x, gin, wg, wu, wd, gfin, *, final, cast=(), tm=512, tf=256, cps=2, ncol=512):
    m, d = x.shape
    dff = wg.shape[1]
    tm = min(tm, m)
    nf = dff // tf
    cps = min(cps, nf)
    n_steps = pl.cdiv(nf, cps)
    assert m % tm == 0 and dff % tf == 0 and d % ncol == 0
    cast_specs = [_cast_spec(a.shape, m // tm, n_steps) for a in cast]

    def chunk_specs(c):
        def col(i, s):
            return (0, jnp.minimum(s * cps + c, nf - 1))

        def row(i, s):
            return (jnp.minimum(s * cps + c, nf - 1), 0)

        return [pl.BlockSpec((d, tf), col), pl.BlockSpec((d, tf), col), pl.BlockSpec((tf, d), row)]

    w_specs = [spec for c in range(cps) for spec in chunk_specs(c)]
    vmem = (tm * d * 4 + 2 * tm * d * 4 + tm * d * 2 + 2 * cps * 3 * d * tf * 2
            + sum(2 * 6 * spec.block_shape[0] * spec.block_shape[1] for spec in cast_specs))
    outs = pl.pallas_call(
        functools.partial(_ffn_kernel, tm=tm, nf=nf, cps=cps, ncol=ncol, n_cast=len(cast), final=final),
        grid=(m // tm, n_steps),
        in_specs=[
            pl.BlockSpec(memory_space=pl.ANY),
            pl.BlockSpec((1, d), lambda i, s: (0, 0)),
            *w_specs,
            pl.BlockSpec((1, d), lambda i, s: (0, 0)),
            *cast_specs,
        ],
        out_specs=[pl.BlockSpec((tm, d), lambda i, s: (i, 0)), *cast_specs],
        out_shape=[jax.ShapeDtypeStruct((m, d), F32), *[jax.ShapeDtypeStruct(a.shape, BF16) for a in cast]],
        scratch_shapes=[pltpu.VMEM((tm, d), F32), pltpu.VMEM((tm, d), BF16), pltpu.VMEM((tm, V7X_LANES), F32),
                        pltpu.SemaphoreType.DMA],
        compiler_params=pltpu.CompilerParams(
            dimension_semantics=("arbitrary", "arbitrary"), vmem_limit_bytes=_vmem_limit(vmem)),
        name="ffn_final" if final else "ffn",
    )(x, gin, *([wg, wu, wd] * cps), gfin, *cast)
    return outs[0] if not cast else tuple(outs)


def _rope_tables(s_len):
    half = ROPE_DIM // 2
    inv = 1.0 / (ROPE_THETA ** (jnp.arange(half, dtype=F32) / half))
    ang = jnp.arange(s_len).astype(F32)[:, None] * inv[None, :]
    cos, sin = jnp.cos(ang), jnp.sin(ang)
    rest = HEAD_DIM - ROPE_DIM
    cos_t = jnp.concatenate([cos, cos, jnp.ones((s_len, rest), F32)], axis=1)
    sin_t = jnp.concatenate([-sin, sin, jnp.zeros((s_len, rest), F32)], axis=1)
    return cos_t, sin_t


def _proj_kernel(x_ref, gin_ref, w_ref, cos_ref, sin_ref, o_ref, h_ref, *, tm, tn, rope_ranges):
    j = pl.program_id(1)
    col = j * tn
    is_rope = functools.reduce(jnp.logical_or, [(col >= lo) & (col < hi) for lo, hi in rope_ranges])

    def rope_step(with_norm):
        gain = gin_ref[...]
        cos = cos_ref[...]
        sin = sin_ref[...]
        half = ROPE_DIM // 2
        rt = tm // ROPE_ROW_SPLIT
        lane = lax.broadcasted_iota(jnp.int32, (rt, HEAD_DIM), 1)
        for r0 in range(0, tm, rt):
            if with_norm:
                for g0 in range(r0, r0 + rt, NORM_ROWS):
                    h_ref[g0:g0 + NORM_ROWS, :] = _rms(x_ref[g0:g0 + NORM_ROWS, :], gain).astype(h_ref.dtype)
            acc = _dot(h_ref[r0:r0 + rt, :], w_ref[...])
            for c in range(0, tn, HEAD_DIM):
                xh = acc[:, c:c + HEAD_DIM]
                partner = jnp.where(lane < half, pltpu.roll(xh, HEAD_DIM - half, 1), pltpu.roll(xh, half, 1))
                rot = xh * cos[r0:r0 + rt] + partner * sin[r0:r0 + rt]
                o_ref[r0:r0 + rt, c:c + HEAD_DIM] = jnp.where(lane < ROPE_DIM, rot, xh).astype(o_ref.dtype)

    pl.when(j == 0)(functools.partial(rope_step, True))
    pl.when(jnp.logical_and(is_rope, j > 0))(functools.partial(rope_step, False))

    @pl.when(jnp.logical_not(is_rope))
    def _():
        o_ref[...] = _dot(h_ref[...], w_ref[...]).astype(o_ref.dtype)


def _proj_in(x, gin, w, cos_t, sin_t, *, s_len, rope_ranges, tm=1024, tn=512):
    m, d = x.shape
    n = w.shape[1]
    tm = min(tm, s_len)
    assert m % tm == 0 and s_len % tm == 0 and n % tn == 0
    assert all(lo % tn == 0 and hi % tn == 0 for lo, hi in rope_ranges) and rope_ranges[0][0] == 0
    per_seq = s_len // tm
    vmem = 2 * tm * d * 4 + tm * d * 2 + 2 * d * tn * 2 + 2 * tm * tn * 2 + 4 * tm * HEAD_DIM * 4
    return pl.pallas_call(
        functools.partial(_proj_kernel, tm=tm, tn=tn, rope_ranges=rope_ranges),
        grid=(m // tm, n // tn),
        in_specs=[
            pl.BlockSpec((tm, d), lambda i, j: (i, 0)),
            pl.BlockSpec((1, d), lambda i, j: (0, 0)),
            pl.BlockSpec((d, tn), lambda i, j: (0, j)),
            pl.BlockSpec((tm, HEAD_DIM), lambda i, j: (i % per_seq, 0)),
            pl.BlockSpec((tm, HEAD_DIM), lambda i, j: (i % per_seq, 0)),
        ],
        out_specs=pl.BlockSpec((tm, tn), lambda i, j: (i, j)),
        out_shape=jax.ShapeDtypeStruct((m, n), BF16),
        scratch_shapes=[pltpu.VMEM((tm, d), BF16)],
        compiler_params=pltpu.CompilerParams(
            dimension_semantics=("parallel", "arbitrary"), vmem_limit_bytes=_vmem_limit(vmem)),
        name="proj_in",
    )(x, gin, w, cos_t, sin_t)


def _attn_a_kernel(sink_ref, q_ref, kp_ref, kc_ref, kn_ref, vp_ref, vc_ref, vn_ref, gain_ref, o_ref, oacc_ref,
                   *, nb, n_kv, group):
    n = pl.program_id(1)
    blk = WINDOW
    rows = group * blk
    rr = lax.broadcasted_iota(jnp.int32, (rows, 3 * blk), 0) & (blk - 1)
    cc = lax.broadcasted_iota(jnp.int32, (rows, 3 * blk), 1)
    lo = jnp.where(n == 0, blk, 0)
    hi = jnp.where(n == nb - 1, 2 * blk, 3 * blk)
    valid = (cc >= rr) & (cc <= rr + 2 * WINDOW) & (cc >= lo) & (cc < hi)
    grp = lax.broadcasted_iota(jnp.int32, (rows, 1), 0) // blk
    for kv in range(n_kv):
        ks = slice(kv * HEAD_DIM, (kv + 1) * HEAD_DIM)
        k = jnp.concatenate([kp_ref[:, ks], kc_ref[:, ks], kn_ref[:, ks]], axis=0)
        v = jnp.concatenate([vp_ref[:, ks], vc_ref[:, ks], vn_ref[:, ks]], axis=0)
        q = jnp.concatenate(
            [q_ref[:, (kv * group + g) * HEAD_DIM:(kv * group + g + 1) * HEAD_DIM] for g in range(group)], axis=0)
        s = _dot_nt(q, k) * (SCALE * LOG2E)
        s = jnp.where(valid, s, NEG)
        sink = jnp.zeros((rows, 1), F32)
        for g in range(group):
            sink = jnp.where(grp == g, sink_ref[kv * group + g] * LOG2E, sink)
        mx = jnp.maximum(jnp.max(s, axis=-1, keepdims=True), sink)
        p = jnp.exp2(s - mx)
        den = jnp.sum(p, axis=-1, keepdims=True) + jnp.exp2(sink - mx)
        o = _dot(p.astype(BF16), v) / den
        for g in range(group):
            h0 = (kv * group + g) * HEAD_DIM
            oacc_ref[:, h0:h0 + HEAD_DIM] = o[g * blk:(g + 1) * blk, :]
    o_ref[...] = _rms(oacc_ref[...], gain_ref[...]).astype(o_ref.dtype)


def _attn_a(proj, sink, gain, *, bsz, s_len, q_col, k_col, v_col, n_q, n_kv):
    m = proj.shape[0]
    blk = WINDOW
    nb = s_len // blk
    qw, kw = n_q * HEAD_DIM, n_kv * HEAD_DIM
    assert s_len % blk == 0 and q_col % qw == 0 and k_col % kw == 0 and v_col % kw == 0
    qb, kb, vb = q_col // qw, k_col // kw, v_col // kw

    def prev_map(col):
        return lambda b, n: (b * nb + jnp.maximum(n - 1, 0), col)

    def cur_map(col):
        return lambda b, n: (b * nb + n, col)

    def next_map(col):
        return lambda b, n: (b * nb + jnp.minimum(n + 1, nb - 1), col)

    return pl.pallas_call(
        functools.partial(_attn_a_kernel, nb=nb, n_kv=n_kv, group=n_q // n_kv),
        grid=(bsz, nb),
        in_specs=[
            pl.BlockSpec(memory_space=pltpu.SMEM),
            pl.BlockSpec((blk, qw), cur_map(qb)),
            pl.BlockSpec((blk, kw), prev_map(kb)),
            pl.BlockSpec((blk, kw), cur_map(kb)),
            pl.BlockSpec((blk, kw), next_map(kb)),
            pl.BlockSpec((blk, kw), prev_map(vb)),
            pl.BlockSpec((blk, kw), cur_map(vb)),
            pl.BlockSpec((blk, kw), next_map(vb)),
            pl.BlockSpec((1, qw), lambda b, n: (0, 0)),
        ],
        out_specs=pl.BlockSpec((blk, qw), lambda b, n: (b * nb + n, 0)),
        out_shape=jax.ShapeDtypeStruct((m, qw), BF16),
        scratch_shapes=[pltpu.VMEM((blk, qw), F32)],
        compiler_params=pltpu.CompilerParams(dimension_semantics=("parallel", "arbitrary")),
        name="attn_a",
    )(sink, proj, proj, proj, proj, proj, proj, proj, gain)


def _bias_tables(rel_bias):
    n_h, n_dr, _ = rel_bias.shape
    col = jnp.arange(GRID_W)
    col_start = jnp.clip(col - NA_KW // 2, 0, GRID_W - NA_KW)
    inside = (col[None, :] >= col_start[:, None]) & (col[None, :] < col_start[:, None] + NA_KW)
    dc = jnp.clip(col[None, :] - col[:, None] + (NA_KW - 1), 0, 2 * NA_KW - 2)
    tc = jnp.where(inside[None, None], rel_bias.astype(F32)[:, :, dc], NEG)
    pairs = jnp.concatenate([tc[:, :-1], tc[:, 1:]], axis=-1)
    return pairs.reshape(n_h * (n_dr - 1), GRID_W, 2 * GRID_W) * LOG2E


def _attn_b_kernel(q_ref, k_ref, v_ref, bias_ref, gain_ref, o_ref, orow_ref, s_ref, p_ref, *, rows, n_h, rpb):
    rb = pl.program_id(1)
    band0 = jnp.clip(rpb * rb - NA_KH // 2, 0, rows - 2 * rpb)
    n_pair = 2 * NA_KH - 2
    gain = gain_ref[...]

    def row_body(ri, carry):
        r = rpb * rb + ri
        r0 = jnp.clip(r - NA_KH // 2, 0, rows - NA_KH)
        off = pl.multiple_of((r0 - band0) * GRID_W, GRID_W)
        qoff = pl.multiple_of(ri * GRID_W, GRID_W)
        dr0 = r0 - r + (NA_KH - 1)
        for h in range(n_h):
            hs = slice(h * HEAD_DIM, (h + 1) * HEAD_DIM)
            q = q_ref[pl.ds(qoff, GRID_W), hs]
            kk = k_ref[pl.ds(off, NA_KH * GRID_W), hs]
            bias = jnp.concatenate([bias_ref[h * n_pair + dr0 + 2 * t] for t in range(NA_KH // 2)], axis=1)
            s_ref[h * GRID_W:(h + 1) * GRID_W, :] = _dot_nt(q, kk) * (SCALE * LOG2E) + bias
        s = s_ref[...]
        mx = jnp.max(s, axis=-1, keepdims=True)
        p = jnp.exp2(s - mx)
        den = jnp.sum(p, axis=-1, keepdims=True)
        p_ref[...] = p.astype(BF16)
        for h in range(n_h):
            hs = slice(h * HEAD_DIM, (h + 1) * HEAD_DIM)
            rs = slice(h * GRID_W, (h + 1) * GRID_W)
            vv = v_ref[pl.ds(off, NA_KH * GRID_W), hs]
            orow_ref[:, hs] = _dot(p_ref[rs, :], vv) / den[rs]
        o_ref[pl.ds(qoff, GRID_W), :] = _rms(orow_ref[...], gain).astype(o_ref.dtype)
        return carry

    lax.fori_loop(0, rpb, row_body, 0)


def _attn_b(proj, bias_t, gain, *, bsz, s_len, q_col, k_col, v_col, n_h, rpb=8):
    m = proj.shape[0]
    rows = s_len // GRID_W
    width = n_h * HEAD_DIM
    assert rows % rpb == 0 and rows >= 2 * rpb and rpb >= NA_KH
    assert all(c % V7X_LANES == 0 for c in (q_col, k_col, v_col))
    nrb = rows // rpb
    band = 2 * rpb * GRID_W
    tq = rpb * GRID_W

    def band_map(col):
        def index(b, rb):
            start = jnp.clip(rpb * rb - NA_KH // 2, 0, rows - 2 * rpb)
            return (pl.multiple_of((b * rows + start) * GRID_W, GRID_W), col)
        return index

    keys = NA_KH * GRID_W
    vmem = (2 * (tq * width * 2 + 2 * band * width * 2 + tq * width * 2) + 2 * bias_t.size * 4 + GRID_W * width * 4
            + n_h * GRID_W * keys * 6)
    return pl.pallas_call(
        functools.partial(_attn_b_kernel, rows=rows, n_h=n_h, rpb=rpb),
        grid=(bsz, nrb),
        in_specs=[
            pl.BlockSpec((pl.Element(tq), pl.Element(width)),
                         lambda b, rb: (pl.multiple_of((b * nrb + rb) * tq, tq), q_col)),
            pl.BlockSpec((pl.Element(band), pl.Element(width)), band_map(k_col)),
            pl.BlockSpec((pl.Element(band), pl.Element(width)), band_map(v_col)),
            pl.BlockSpec(bias_t.shape, lambda b, rb: (0, 0, 0)),
            pl.BlockSpec((1, width), lambda b, rb: (0, 0)),
        ],
        out_specs=pl.BlockSpec((tq, width), lambda b, rb: (b * nrb + rb, 0)),
        out_shape=jax.ShapeDtypeStruct((m, width), BF16),
        scratch_shapes=[pltpu.VMEM((GRID_W, width), F32), pltpu.VMEM((n_h * GRID_W, keys), F32),
                        pltpu.VMEM((n_h * GRID_W, keys), BF16)],
        compiler_params=pltpu.CompilerParams(
            dimension_semantics=("parallel", "arbitrary"), vmem_limit_bytes=_vmem_limit(vmem)),
        name="attn_b",
    )(proj, proj, proj, bias_t, gain)


def _out_proj_kernel(a_ref, b_ref, wa_ref, wb_ref, x_ref, o_ref):
    o_ref[...] = x_ref[...] + (_dot(a_ref[...], wa_ref[...]) + _dot(b_ref[...], wb_ref[...]))


def _out_proj(oa, ob, w, x, *, tm=1024, tn=512):
    m, d = x.shape
    ka, kb = oa.shape[1], ob.shape[1]
    tm = min(tm, m)
    assert ka == kb and m % tm == 0 and d % tn == 0
    vmem = 2 * (2 * tm * ka * 2 + 2 * ka * tn * 2 + 2 * tm * tn * 4)
    return pl.pallas_call(
        _out_proj_kernel,
        grid=(m // tm, d // tn),
        in_specs=[
            pl.BlockSpec((tm, ka), lambda i, j: (i, 0)),
            pl.BlockSpec((tm, kb), lambda i, j: (i, 0)),
            pl.BlockSpec((ka, tn), lambda i, j: (0, j)),
            pl.BlockSpec((kb, tn), lambda i, j: (1, j)),
            pl.BlockSpec((tm, tn), lambda i, j: (i, j)),
        ],
        out_specs=pl.BlockSpec((tm, tn), lambda i, j: (i, j)),
        out_shape=jax.ShapeDtypeStruct((m, d), F32),
        compiler_params=pltpu.CompilerParams(
            dimension_semantics=("parallel", "arbitrary"), vmem_limit_bytes=_vmem_limit(vmem)),
        name="out_proj",
    )(oa, ob, w, w, x)


def _mem_kv_kernel(mem_ref, gain_ref, w_ref, o_ref, h_ref, *, tm):
    _norm_rows(mem_ref, gain_ref, h_ref, tm)
    o_ref[...] = _dot(h_ref[...], w_ref[...]).astype(o_ref.dtype)


def _mem_kv(mem, gain, w, *, tm=256):
    m, d = mem.shape
    n = w.shape[1]
    tm = min(tm, m)
    assert m % tm == 0
    vmem = 2 * tm * d * 4 + tm * d * 2 + 2 * d * n * 2 + 2 * tm * n * 2
    return pl.pallas_call(
        functools.partial(_mem_kv_kernel, tm=tm),
        grid=(m // tm,),
        in_specs=[
            pl.BlockSpec((tm, d), lambda i: (i, 0)),
            pl.BlockSpec((1, d), lambda i: (0, 0)),
            pl.BlockSpec((d, n), lambda i: (0, 0)),
        ],
        out_specs=pl.BlockSpec((tm, n), lambda i: (i, 0)),
        out_shape=jax.ShapeDtypeStruct((m, n), BF16),
        scratch_shapes=[pltpu.VMEM((tm, d), BF16)],
        compiler_params=pltpu.CompilerParams(
            dimension_semantics=("parallel",), vmem_limit_bytes=_vmem_limit(vmem)),
        name="mem_kv",
    )(mem, gain, w)


def _cross_kernel(x_ref, gain_ref, wq_ref, k_ref, v_ref, wo_ref, o_ref, h_ref, att_ref, *, tm):
    gain = gain_ref[...]
    rt = tm // CROSS_ROW_SPLIT
    for r0 in range(0, tm, rt):
        for g0 in range(r0, r0 + rt, NORM_ROWS):
            h_ref[g0:g0 + NORM_ROWS, :] = _rms(x_ref[g0:g0 + NORM_ROWS, :], gain).astype(h_ref.dtype)
        q = _dot(h_ref[r0:r0 + rt, :], wq_ref[...]).astype(BF16)
        for h in range(CA_HEADS):
            hs = slice(h * HEAD_DIM, (h + 1) * HEAD_DIM)
            s = _dot_nt(q[:, hs], k_ref[:, hs]) * (SCALE * LOG2E)
            mx = jnp.max(s, axis=-1, keepdims=True)
            p = jnp.exp2(s - mx)
            den = jnp.sum(p, axis=-1, keepdims=True)
            att_ref[r0:r0 + rt, hs] = (_dot(p.astype(BF16), v_ref[:, hs]) / den).astype(BF16)
        o_ref[r0:r0 + rt, :] = x_ref[r0:r0 + rt, :] + _dot(att_ref[r0:r0 + rt, :], wo_ref[...])


def _cross(x, gain, wq, kv, wo, *, s_len, n_mem, tm=512):
    m, d = x.shape
    caw = wq.shape[1]
    tm = min(tm, s_len)
    assert m % tm == 0 and s_len % tm == 0 and caw == CA_HEADS * HEAD_DIM
    per_seq = s_len // tm
    vmem = 4 * tm * d * 4 + tm * d * 2 + 2 * 2 * d * caw * 2 + 4 * n_mem * caw * 2 + tm * caw * 2
    return pl.pallas_call(
        functools.partial(_cross_kernel, tm=tm),
        grid=(m // tm,),
        in_specs=[
            pl.BlockSpec((tm, d), lambda i: (i, 0)),
            pl.BlockSpec((1, d), lambda i: (0, 0)),
            pl.BlockSpec((d, caw), lambda i: (0, 0)),
            pl.BlockSpec((n_mem, caw), lambda i: (i // per_seq, 0)),
            pl.BlockSpec((n_mem, caw), lambda i: (i // per_seq, 1)),
            pl.BlockSpec((caw, d), lambda i: (0, 0)),
        ],
        out_specs=pl.BlockSpec((tm, d), lambda i: (i, 0)),
        out_shape=jax.ShapeDtypeStruct((m, d), F32),
        scratch_shapes=[pltpu.VMEM((tm, d), BF16), pltpu.VMEM((tm, caw), BF16)],
        compiler_params=pltpu.CompilerParams(
            dimension_semantics=("parallel",), vmem_limit_bytes=_vmem_limit(vmem)),
        name="cross",
    )(x, gain, wq, kv, kv, wo)


def _trunk(x3, mem3, p, late=None):
    bsz, s_len, d = x3.shape
    n_mem = mem3.shape[1]
    x = x3.reshape(bsz * s_len, d)
    mem = mem3.reshape(bsz * n_mem, d)
    cos_t, sin_t = _rope_tables(s_len)

    ffn1 = functools.partial(_ffn, x, p["ffn1_norm"], p["ffn1_w_gate"], p["ffn1_w_up"], p["ffn1_w_down"],
                             p["final_norm"], final=False)
    if late is None:
        x, *cast = ffn1(cast=tuple(p["late_f32"].values()))
        late = dict(zip(p["late_f32"], cast))
    else:
        x = ffn1()
    proj = _proj_in(x, p["mix_norm"], late["w_in"], cos_t, sin_t, s_len=s_len, rope_ranges=p["rope_ranges"])
    c = p["cols"]
    oa = _attn_a(proj, p["a_sink"], p["a_out_norm"], bsz=bsz, s_len=s_len,
                 q_col=c["qa"], k_col=c["ka"], v_col=c["va"], n_q=p["a_q_heads"], n_kv=p["a_kv_heads"])
    ob = _attn_b(proj, p["b_bias"], p["b_out_norm"], bsz=bsz, s_len=s_len,
                 q_col=c["qb"], k_col=c["kb"], v_col=c["vb"], n_h=p["b_heads"])
    x = _out_proj(oa, ob, late["w_out"], x)
    kv = _mem_kv(mem, p["mem_norm"], p["ca_w_kv"])
    x = _cross(x, p["ca_norm"], p["ca_w_q"], kv, p["ca_w_o"], s_len=s_len, n_mem=n_mem)
    y = _ffn(x, p["ffn2_norm"], late["ffn2_w_gate"], late["ffn2_w_up"], late["ffn2_w_down"], p["final_norm"],
             final=True)
    return y.reshape(bsz, s_len, d), late


def _prepare(ffn1_norm, ffn1_w_gate, ffn1_w_up, ffn1_w_down, mix_norm, w_in, a_sink, b_rel_bias, a_out_norm,
             b_out_norm, w_out, ca_norm, mem_norm, ca_w_q, ca_w_kv, ca_w_o, ffn2_norm, ffn2_w_gate, ffn2_w_up,
             ffn2_w_down, final_norm):
    d = w_in.shape[1]
    n_heads = d // HEAD_DIM
    a_q = n_heads // 2
    a_kv = a_q // 4
    b_h = n_heads - a_q
    aq_w, akv_w, b_w = a_q * HEAD_DIM, a_kv * HEAD_DIM, b_h * HEAD_DIM
    cols = {"qa": 0, "ka": aq_w, "va": aq_w + akv_w, "qb": aq_w + 2 * akv_w, "kb": aq_w + 2 * akv_w + b_w,
            "vb": aq_w + 2 * akv_w + 2 * b_w}

    def gain(g):
        return g.reshape(1, -1).astype(F32)

    def wt(a):
        return a[0].astype(BF16)

    return {
        "ffn1_norm": gain(ffn1_norm[0]), "ffn1_w_gate": wt(ffn1_w_gate), "ffn1_w_up": wt(ffn1_w_up),
        "ffn1_w_down": wt(ffn1_w_down),
        "mix_norm": gain(mix_norm[0]), "cols": cols,
        "rope_ranges": ((cols["qa"], cols["qa"] + aq_w), (cols["ka"], cols["ka"] + akv_w)),
        "a_sink": a_sink[0].astype(F32), "b_bias": _bias_tables(b_rel_bias[0]),
        "a_out_norm": gain(a_out_norm[0]), "b_out_norm": gain(b_out_norm[0]),
        "a_q_heads": a_q, "a_kv_heads": a_kv, "b_heads": b_h,
        "ca_norm": gain(ca_norm[0]), "mem_norm": gain(mem_norm[0]), "ca_w_q": wt(ca_w_q), "ca_w_kv": wt(ca_w_kv),
        "ca_w_o": wt(ca_w_o),
        "ffn2_norm": gain(ffn2_norm[0]), "final_norm": gain(final_norm),
        "late_f32": {"w_in": w_in[0], "w_out": w_out[0], "ffn2_w_gate": ffn2_w_gate[0], "ffn2_w_up": ffn2_w_up[0],
                     "ffn2_w_down": ffn2_w_down[0]},
    }


def kernel(x_prompt, x_sample, mem_prompt, mem_sample, ffn1_norm, ffn1_w_gate, ffn1_w_up, ffn1_w_down, mix_norm, w_in, a_sink, b_rel_bias, a_out_norm, b_out_norm, w_out, ca_norm, mem_norm, ca_w_q, ca_w_kv, ca_w_o, ffn2_norm, ffn2_w_gate, ffn2_w_up, ffn2_w_down, final_norm):
    assert ffn1_w_gate.shape[0] == 1, "single-layer trunk"
    p = _prepare(ffn1_norm, ffn1_w_gate, ffn1_w_up, ffn1_w_down, mix_norm, w_in, a_sink, b_rel_bias, a_out_norm,
                 b_out_norm, w_out, ca_norm, mem_norm, ca_w_q, ca_w_kv, ca_w_o, ffn2_norm, ffn2_w_gate, ffn2_w_up,
                 ffn2_w_down, final_norm)
    y_prompt, late = _trunk(x_prompt, mem_prompt, p)
    y_sample, _ = _trunk(x_sample, mem_sample, p, late)
    return (y_prompt, y_sample)
```

```python
import functools

import jax
import jax.numpy as jnp
from jax import lax
from jax.experimental import pallas as pl
from jax.experimental.pallas import tpu as pltpu

F32 = jnp.float32
BF16 = jnp.bfloat16

HEAD_DIM = 128
WINDOW = 128
ROPE_THETA = 500000.0
ROPE_DIM = HEAD_DIM // 4
GRID_W = 64
NA_KH = 8
NA_KW = 16
CA_HEADS = 4
EPS = 1e-6
NEG = -1e30
SCALE = HEAD_DIM ** -0.5
LOG2E = 1.4426950408889634

V7X_VMEM_BYTES = 64 * 1024 * 1024
V7X_LANES = 128
ROPE_ROW_SPLIT = 4
CROSS_ROW_SPLIT = 2
CAST_ROWS = 16
NORM_ROWS = 16
NORM_UNROLL = 4


def _vmem_limit(nbytes):
    return int(min(nbytes + 12 * 1024 * 1024, V7X_VMEM_BYTES - 4 * 1024 * 1024))


def _rms(x, gain):
    ms = jnp.mean(x * x, axis=-1, keepdims=True)
    return x * lax.rsqrt(ms + EPS) * gain


def _norm_rows(src_ref, gain_ref, dst_ref, rows):
    gain = gain_ref[...]

    def body(i, carry):
        r = pl.multiple_of(i * NORM_ROWS, NORM_ROWS)
        x = src_ref[pl.ds(r, NORM_ROWS), :]
        dst_ref[pl.ds(r, NORM_ROWS), :] = _rms(x, gain).astype(dst_ref.dtype)
        return carry

    lax.fori_loop(0, rows // NORM_ROWS, body, 0, unroll=NORM_UNROLL)


def _dot(a, b):
    return jnp.dot(a, b, preferred_element_type=F32)


def _dot_nt(a, b):
    return lax.dot_general(a, b, (((1,), (1,)), ((), ())), preferred_element_type=F32)


def _ffn_kernel(x_hbm, gin_ref, *refs, tm, nf, cps, ncol, n_cast, final):
    w_refs, refs = refs[:3 * cps], refs[3 * cps:]
    gfin_ref, cast_in, refs = refs[0], refs[1:1 + n_cast], refs[1 + n_cast:]
    o_ref, cast_out, (xbuf, h_ref, rs_ref, sem) = refs[0], refs[1:1 + n_cast], refs[1 + n_cast:]
    i = pl.program_id(0)
    s = pl.program_id(1)
    n_tiles = pl.num_programs(0)
    n_steps = pl.num_programs(1)
    d = o_ref.shape[1]

    def x_copy(tile):
        return pltpu.make_async_copy(x_hbm.at[pl.ds(pl.multiple_of(tile * tm, tm), tm), :], xbuf, sem)

    @pl.when(s == 0)
    def _():
        @pl.when(i == 0)
        def _():
            x_copy(0).start()

        x_copy(i).wait()
        gin = gin_ref[...]

        def first(j, carry):
            rows = pl.ds(pl.multiple_of(j * NORM_ROWS, NORM_ROWS), NORM_ROWS)
            x = xbuf[rows, :]
            h_ref[rows, :] = _rms(x, gin).astype(h_ref.dtype)
            o_ref[rows, :] = x
            return carry

        lax.fori_loop(0, tm // NORM_ROWS, first, 0, unroll=NORM_UNROLL)

        @pl.when(i + 1 < n_tiles)
        def _():
            x_copy(i + 1).start()

    def chunk(wg_ref, wu_ref, wd_ref, row_scales):
        h = h_ref[...]
        g = _dot(h, wg_ref[...])
        u = _dot(h, wu_ref[...])
        act = (0.5 * (g / (1.0 + jnp.exp(-g)) * u)).astype(BF16)
        sumsq = jnp.zeros((tm, 1), F32)
        for c in range(0, d, ncol):
            y = o_ref[:, c:c + ncol] + _dot(act, wd_ref[:, c:c + ncol])
            o_ref[:, c:c + ncol] = y
            if row_scales:
                sumsq = sumsq + jnp.sum(y * y, axis=-1, keepdims=True)
        if row_scales:
            rs_ref[...] = jnp.broadcast_to(lax.rsqrt(sumsq / d + EPS), rs_ref.shape)

    def chunks(count, last_step):
        for c in range(count):
            chunk(*w_refs[3 * c:3 * c + 3], final and last_step and c == count - 1)
        for src, dst in zip(cast_in, cast_out):
            dst[...] = src[...].astype(dst.dtype)

    tail = nf - (pl.cdiv(nf, cps) - 1) * cps
    pl.when(s < n_steps - 1)(functools.partial(chunks, cps, False))
    pl.when(s == n_steps - 1)(functools.partial(chunks, tail, True))

    if final:
        @pl.when(s == n_steps - 1)
        def _():
            gfin = gfin_ref[...]

            def scale(j, carry):
                rows = pl.ds(pl.multiple_of(j * NORM_ROWS, NORM_ROWS), NORM_ROWS)
                o_ref[rows, :] = o_ref[rows, :] * rs_ref[rows, :][:, :1] * gfin
                return carry

            lax.fori_loop(0, tm // NORM_ROWS, scale, 0, unroll=NORM_UNROLL)


def _cast_spec(shape, n_tiles, n_steps):
    r, c = shape
    grid_steps = n_tiles * n_steps
    assert r % CAST_ROWS == 0
    groups = r // CAST_ROWS
    per_block = min(k for k in range(1, groups + 1) if groups % k == 0 and groups // k <= grid_steps)
    row_blocks = groups // per_block
    splits = 1
    while row_blocks * splits * 2 <= grid_steps and c % (splits * 2 * V7X_LANES) == 0:
        splits *= 2
    n_blocks = row_blocks * splits

    def index(i, s):
        t = jnp.minimum(i * n_steps + s, n_blocks - 1)
        return (t // splits, t % splits)

    return pl.BlockSpec((per_block * CAST_ROWS, c // splits), index)


def _ffn(x, gin, wg, wu, wd, gfin, *, final, cast=(), tm=512, tf=256, cps=2, ncol=512):
    m, d = x.shape
    dff = wg.shape[1]
    tm = min(tm, m)
    nf = dff // tf
    cps = min(cps, nf)
    n_steps = pl.cdiv(nf, cps)
    assert m % tm == 0 and dff % tf == 0 and d % ncol == 0
    cast_specs = [_cast_spec(a.shape, m // tm, n_steps) for a in cast]

    def chunk_specs(c):
        def col(i, s):
            return (0, jnp.minimum(s * cps + c, nf - 1))

        def row(i, s):
            return (jnp.minimum(s * cps + c, nf - 1), 0)

        return [pl.BlockSpec((d, tf), col), pl.BlockSpec((d, tf), col), pl.BlockSpec((tf, d), row)]

    w_specs = [spec for c in range(cps) for spec in chunk_specs(c)]
    vmem = (tm * d * 4 + 2 * tm * d * 4 + tm * d * 2 + 2 * cps * 3 * d * tf * 2
            + sum(2 * 6 * spec.block_shape[0] * spec.block_shape[1] for spec in cast_specs))
    outs = pl.pallas_call(
        functools.partial(_ffn_kernel, tm=tm, nf=nf, cps=cps, ncol=ncol, n_cast=len(cast), final=final),
        grid=(m // tm, n_steps),
        in_specs=[
            pl.BlockSpec(memory_space=pl.ANY),
            pl.BlockSpec((1, d), lambda i, s: (0, 0)),
            *w_specs,
            pl.BlockSpec((1, d), lambda i, s: (0, 0)),
            *cast_specs,
        ],
        out_specs=[pl.BlockSpec((tm, d), lambda i, s: (i, 0)), *cast_specs],
        out_shape=[jax.ShapeDtypeStruct((m, d), F32), *[jax.ShapeDtypeStruct(a.shape, BF16) for a in cast]],
        scratch_shapes=[pltpu.VMEM((tm, d), F32), pltpu.VMEM((tm, d), BF16), pltpu.VMEM((tm, V7X_LANES), F32),
                        pltpu.SemaphoreType.DMA],
        compiler_params=pltpu.CompilerParams(
            dimension_semantics=("arbitrary", "arbitrary"), vmem_limit_bytes=_vmem_limit(vmem)),
        name="ffn_final" if final else "ffn",
    )(x, gin, *([wg, wu, wd] * cps), gfin, *cast)
    return outs[0] if not cast else tuple(outs)


def _rope_tables(s_len):
    half = ROPE_DIM // 2
    inv = 1.0 / (ROPE_THETA ** (jnp.arange(half, dtype=F32) / half))
    ang = jnp.arange(s_len).astype(F32)[:, None] * inv[None, :]
    cos, sin = jnp.cos(ang), jnp.sin(ang)
    rest = HEAD_DIM - ROPE_DIM
    cos_t = jnp.concatenate([cos, cos, jnp.ones((s_len, rest), F32)], axis=1)
    sin_t = jnp.concatenate([-sin, sin, jnp.zeros((s_len, rest), F32)], axis=1)
    return cos_t, sin_t


def _proj_kernel(x_ref, gin_ref, w_ref, cos_ref, sin_ref, o_ref, h_ref, *, tm, tn, rope_ranges):
    j = pl.program_id(1)
    col = j * tn
    is_rope = functools.reduce(jnp.logical_or, [(col >= lo) & (col < hi) for lo, hi in rope_ranges])

    def rope_step(with_norm):
        gain = gin_ref[...]
        cos = cos_ref[...]
        sin = sin_ref[...]
        half = ROPE_DIM // 2
        rt = tm // ROPE_ROW_SPLIT
        lane = lax.broadcasted_iota(jnp.int32, (rt, HEAD_DIM), 1)
        for r0 in range(0, tm, rt):
            if with_norm:
                for g0 in range(r0, r0 + rt, NORM_ROWS):
                    h_ref[g0:g0 + NORM_ROWS, :] = _rms(x_ref[g0:g0 + NORM_ROWS, :], gain).astype(h_ref.dtype)
            acc = _dot(h_ref[r0:r0 + rt, :], w_ref[...])
            for c in range(0, tn, HEAD_DIM):
                xh = acc[:, c:c + HEAD_DIM]
                partner = jnp.where(lane < half, pltpu.roll(xh, HEAD_DIM - half, 1), pltpu.roll(xh, half, 1))
                rot = xh * cos[r0:r0 + rt] + partner * sin[r0:r0 + rt]
                o_ref[r0:r0 + rt, c:c + HEAD_DIM] = jnp.where(lane < ROPE_DIM, rot, xh).astype(o_ref.dtype)

    pl.when(j == 0)(functools.partial(rope_step, True))
    pl.when(jnp.logical_and(is_rope, j > 0))(functools.partial(rope_step, False))

    @pl.when(jnp.logical_not(is_rope))
    def _():
        o_ref[...] = _dot(h_ref[...], w_ref[...]).astype(o_ref.dtype)


def _proj_in(x, gin, w, cos_t, sin_t, *, s_len, rope_ranges, tm=1024, tn=512):
    m, d = x.shape
    n = w.shape[1]
    tm = min(tm, s_len)
    assert m % tm == 0 and s_len % tm == 0 and n % tn == 0
    assert all(lo % tn == 0 and hi % tn == 0 for lo, hi in rope_ranges) and rope_ranges[0][0] == 0
    per_seq = s_len // tm
    vmem = 2 * tm * d * 4 + tm * d * 2 + 2 * d * tn * 2 + 2 * tm * tn * 2 + 4 * tm * HEAD_DIM * 4
    return pl.pallas_call(
        functools.partial(_proj_kernel, tm=tm, tn=tn, rope_ranges=rope_ranges),
        grid=(m // tm, n // tn),
        in_specs=[
            pl.BlockSpec((tm, d), lambda i, j: (i, 0)),
            pl.BlockSpec((1, d), lambda i, j: (0, 0)),
            pl.BlockSpec((d, tn), lambda i, j: (0, j)),
            pl.BlockSpec((tm, HEAD_DIM), lambda i, j: (i % per_seq, 0)),
            pl.BlockSpec((tm, HEAD_DIM), lambda i, j: (i % per_seq, 0)),
        ],
        out_specs=pl.BlockSpec((tm, tn), lambda i, j: (i, j)),
        out_shape=jax.ShapeDtypeStruct((m, n), BF16),
        scratch_shapes=[pltpu.VMEM((tm, d), BF16)],
        compiler_params=pltpu.CompilerParams(
            dimension_semantics=("parallel", "arbitrary"), vmem_limit_bytes=_vmem_limit(vmem)),
        name="proj_in",
    )(x, gin, w, cos_t, sin_t)


def _attn_a_kernel(sink_ref, q_ref, kp_ref, kc_ref, kn_ref, vp_ref, vc_ref, vn_ref, gain_ref, o_ref, oacc_ref,
                   *, nb, n_kv, group):
    n = pl.program_id(1)
    blk = WINDOW
    rows = group * blk
    rr = lax.broadcasted_iota(jnp.int32, (rows, 3 * blk), 0) & (blk - 1)
    cc = lax.broadcasted_iota(jnp.int32, (rows, 3 * blk), 1)
    lo = jnp.where(n == 0, blk, 0)
    hi = jnp.where(n == nb - 1, 2 * blk, 3 * blk)
    valid = (cc >= rr) & (cc <= rr + 2 * WINDOW) & (cc >= lo) & (cc < hi)
    grp = lax.broadcasted_iota(jnp.int32, (rows, 1), 0) // blk
    for kv in range(n_kv):
        ks = slice(kv * HEAD_DIM, (kv + 1) * HEAD_DIM)
        k = jnp.concatenate([kp_ref[:, ks], kc_ref[:, ks], kn_ref[:, ks]], axis=0)
        v = jnp.concatenate([vp_ref[:, ks], vc_ref[:, ks], vn_ref[:, ks]], axis=0)
        q = jnp.concatenate(
            [q_ref[:, (kv * group + g) * HEAD_DIM:(kv * group + g + 1) * HEAD_DIM] for g in range(group)], axis=0)
        s = _dot_nt(q, k) * (SCALE * LOG2E)
        s = jnp.where(valid, s, NEG)
        sink = jnp.zeros((rows, 1), F32)
        for g in range(group):
            sink = jnp.where(grp == g, sink_ref[kv * group + g] * LOG2E, sink)
        mx = jnp.maximum(jnp.max(s, axis=-1, keepdims=True), sink)
        p = jnp.exp2(s - mx)
        den = jnp.sum(p, axis=-1, keepdims=True) + jnp.exp2(sink - mx)
        o = _dot(p.astype(BF16), v) / den
        for g in range(group):
            h0 = (kv * group + g) * HEAD_DIM
            oacc_ref[:, h0:h0 + HEAD_DIM] = o[g * blk:(g + 1) * blk, :]
    o_ref[...] = _rms(oacc_ref[...], gain_ref[...]).astype(o_ref.dtype)


def _attn_a(proj, sink, gain, *, bsz, s_len, q_col, k_col, v_col, n_q, n_kv):
    m = proj.shape[0]
    blk = WINDOW
    nb = s_len // blk
    qw, kw = n_q * HEAD_DIM, n_kv * HEAD_DIM
    assert s_len % blk == 0 and q_col % qw == 0 and k_col % kw == 0 and v_col % kw == 0
    qb, kb, vb = q_col // qw, k_col // kw, v_col // kw

    def prev_map(col):
        return lambda b, n: (b * nb + jnp.maximum(n - 1, 0), col)

    def cur_map(col):
        return lambda b, n: (b * nb + n, col)

    def next_map(col):
        return lambda b, n: (b * nb + jnp.minimum(n + 1, nb - 1), col)

    return pl.pallas_call(
        functools.partial(_attn_a_kernel, nb=nb, n_kv=n_kv, group=n_q // n_kv),
        grid=(bsz, nb),
        in_specs=[
            pl.BlockSpec(memory_space=pltpu.SMEM),
            pl.BlockSpec((blk, qw), cur_map(qb)),
            pl.BlockSpec((blk, kw), prev_map(kb)),
            pl.BlockSpec((blk, kw), cur_map(kb)),
            pl.BlockSpec((blk, kw), next_map(kb)),
            pl.BlockSpec((blk, kw), prev_map(vb)),
            pl.BlockSpec((blk, kw), cur_map(vb)),
            pl.BlockSpec((blk, kw), next_map(vb)),
            pl.BlockSpec((1, qw), lambda b, n: (0, 0)),
        ],
        out_specs=pl.BlockSpec((blk, qw), lambda b, n: (b * nb + n, 0)),
        out_shape=jax.ShapeDtypeStruct((m, qw), BF16),
        scratch_shapes=[pltpu.VMEM((blk, qw), F32)],
        compiler_params=pltpu.CompilerParams(dimension_semantics=("parallel", "arbitrary")),
        name="attn_a",
    )(sink, proj, proj, proj, proj, proj, proj, proj, gain)


def _bias_tables(rel_bias):
    n_h, n_dr, _ = rel_bias.shape
    col = jnp.arange(GRID_W)
    col_start = jnp.clip(col - NA_KW // 2, 0, GRID_W - NA_KW)
    inside = (col[None, :] >= col_start[:, None]) & (col[None, :] < col_start[:, None] + NA_KW)
    dc = jnp.clip(col[None, :] - col[:, None] + (NA_KW - 1), 0, 2 * NA_KW - 2)
    tc = jnp.where(inside[None, None], rel_bias.astype(F32)[:, :, dc], NEG)
    pairs = jnp.concatenate([tc[:, :-1], tc[:, 1:]], axis=-1)
    return pairs.reshape(n_h * (n_dr - 1), GRID_W, 2 * GRID_W) * LOG2E


def _attn_b_kernel(q_ref, k_ref, v_ref, bias_ref, gain_ref, o_ref, orow_ref, s_ref, p_ref, *, rows, n_h, rpb):
    rb = pl.program_id(1)
    band0 = jnp.clip(rpb * rb - NA_KH // 2, 0, rows - 2 * rpb)
    n_pair = 2 * NA_KH - 2
    gain = gain_ref[...]

    def row_body(ri, carry):
        r = rpb * rb + ri
        r0 = jnp.clip(r - NA_KH // 2, 0, rows - NA_KH)
        off = pl.multiple_of((r0 - band0) * GRID_W, GRID_W)
        qoff = pl.multiple_of(ri * GRID_W, GRID_W)
        dr0 = r0 - r + (NA_KH - 1)
        for h in range(n_h):
            hs = slice(h * HEAD_DIM, (h + 1) * HEAD_DIM)
            q = q_ref[pl.ds(qoff, GRID_W), hs]
            kk = k_ref[pl.ds(off, NA_KH * GRID_W), hs]
            bias = jnp.concatenate([bias_ref[h * n_pair + dr0 + 2 * t] for t in range(NA_KH // 2)], axis=1)
            s_ref[h * GRID_W:(h + 1) * GRID_W, :] = _dot_nt(q, kk) * (SCALE * LOG2E) + bias
        s = s_ref[...]
        mx = jnp.max(s, axis=-1, keepdims=True)
        p = jnp.exp2(s - mx)
        den = jnp.sum(p, axis=-1, keepdims=True)
        p_ref[...] = p.astype(BF16)
        for h in range(n_h):
            hs = slice(h * HEAD_DIM, (h + 1) * HEAD_DIM)
            rs = slice(h * GRID_W, (h + 1) * GRID_W)
            vv = v_ref[pl.ds(off, NA_KH * GRID_W), hs]
            orow_ref[:, hs] = _dot(p_ref[rs, :], vv) / den[rs]
        o_ref[pl.ds(qoff, GRID_W), :] = _rms(orow_ref[...], gain).astype(o_ref.dtype)
        return carry

    lax.fori_loop(0, rpb, row_body, 0)


def _attn_b(proj, bias_t, gain, *, bsz, s_len, q_col, k_col, v_col, n_h, rpb=8):
    m = proj.shape[0]
    rows = s_len // GRID_W
    width = n_h * HEAD_DIM
    assert rows % rpb == 0 and rows >= 2 * rpb and rpb >= NA_KH
    assert all(c % V7X_LANES == 0 for c in (q_col, k_col, v_col))
    nrb = rows // rpb
    band = 2 * rpb * GRID_W
    tq = rpb * GRID_W

    def band_map(col):
        def index(b, rb):
            start = jnp.clip(rpb * rb - NA_KH // 2, 0, rows - 2 * rpb)
            return (pl.multiple_of((b * rows + start) * GRID_W, GRID_W), col)
        return index

    keys = NA_KH * GRID_W
    vmem = (2 * (tq * width * 2 + 2 * band * width * 2 + tq * width * 2) + 2 * bias_t.size * 4 + GRID_W * width * 4
            + n_h * GRID_W * keys * 6)
    return pl.pallas_call(
        functools.partial(_attn_b_kernel, rows=rows, n_h=n_h, rpb=rpb),
        grid=(bsz, nrb),
        in_specs=[
            pl.BlockSpec((pl.Element(tq), pl.Element(width)),
                         lambda b, rb: (pl.multiple_of((b * nrb + rb) * tq, tq), q_col)),
            pl.BlockSpec((pl.Element(band), pl.Element(width)), band_map(k_col)),
            pl.BlockSpec((pl.Element(band), pl.Element(width)), band_map(v_col)),
            pl.BlockSpec(bias_t.shape, lambda b, rb: (0, 0, 0)),
            pl.BlockSpec((1, width), lambda b, rb: (0, 0)),
        ],
        out_specs=pl.BlockSpec((tq, width), lambda b, rb: (b * nrb + rb, 0)),
        out_shape=jax.ShapeDtypeStruct((m, width), BF16),
        scratch_shapes=[pltpu.VMEM((GRID_W, width), F32), pltpu.VMEM((n_h * GRID_W, keys), F32),
                        pltpu.VMEM((n_h * GRID_W, keys), BF16)],
        compiler_params=pltpu.CompilerParams(
            dimension_semantics=("parallel", "arbitrary"), vmem_limit_bytes=_vmem_limit(vmem)),
        name="attn_b",
    )(proj, proj, proj, bias_t, gain)


def _out_proj_kernel(a_ref, b_ref, wa_ref, wb_ref, x_ref, o_ref):
    o_ref[...] = x_ref[...] + (_dot(a_ref[...], wa_ref[...]) + _dot(b_ref[...], wb_ref[...]))


def _out_proj(oa, ob, w, x, *, tm=1024, tn=512):
    m, d = x.shape
    ka, kb = oa.shape[1], ob.shape[1]
    tm = min(tm, m)
    assert ka == kb and m % tm == 0 and d % tn == 0
    vmem = 2 * (2 * tm * ka * 2 + 2 * ka * tn * 2 + 2 * tm * tn * 4)
    return pl.pallas_call(
        _out_proj_kernel,
        grid=(m // tm, d // tn),
        in_specs=[
            pl.BlockSpec((tm, ka), lambda i, j: (i, 0)),
            pl.BlockSpec((tm, kb), lambda i, j: (i, 0)),
            pl.BlockSpec((ka, tn), lambda i, j: (0, j)),
            pl.BlockSpec((kb, tn), lambda i, j: (1, j)),
            pl.BlockSpec((tm, tn), lambda i, j: (i, j)),
        ],
        out_specs=pl.BlockSpec((tm, tn), lambda i, j: (i, j)),
        out_shape=jax.ShapeDtypeStruct((m, d), F32),
        compiler_params=pltpu.CompilerParams(
            dimension_semantics=("parallel", "arbitrary"), vmem_limit_bytes=_vmem_limit(vmem)),
        name="out_proj",
    )(oa, ob, w, w, x)


def _mem_kv_kernel(mem_ref, gain_ref, w_ref, o_ref, h_ref, *, tm):
    _norm_rows(mem_ref, gain_ref, h_ref, tm)
    o_ref[...] = _dot(h_ref[...], w_ref[...]).astype(o_ref.dtype)


def _mem_kv(mem, gain, w, *, tm=256):
    m, d = mem.shape
    n = w.shape[1]
    tm = min(tm, m)
    assert m % tm == 0
    vmem = 2 * tm * d * 4 + tm * d * 2 + 2 * d * n * 2 + 2 * tm * n * 2
    return pl.pallas_call(
        functools.partial(_mem_kv_kernel, tm=tm),
        grid=(m // tm,),
        in_specs=[
            pl.BlockSpec((tm, d), lambda i: (i, 0)),
            pl.BlockSpec((1, d), lambda i: (0, 0)),
            pl.BlockSpec((d, n), lambda i: (0, 0)),
        ],
        out_specs=pl.BlockSpec((tm, n), lambda i: (i, 0)),
        out_shape=jax.ShapeDtypeStruct((m, n), BF16),
        scratch_shapes=[pltpu.VMEM((tm, d), BF16)],
        compiler_params=pltpu.CompilerParams(
            dimension_semantics=("parallel",), vmem_limit_bytes=_vmem_limit(vmem)),
        name="mem_kv",
    )(mem, gain, w)


def _cross_kernel(x_ref, gain_ref, wq_ref, k_ref, v_ref, wo_ref, o_ref, h_ref, att_ref, *, tm):
    gain = gain_ref[...]
    rt = tm // CROSS_ROW_SPLIT
    for r0 in range(0, tm, rt):
        for g0 in range(r0, r0 + rt, NORM_ROWS):
            h_ref[g0:g0 + NORM_ROWS, :] = _rms(x_ref[g0:g0 + NORM_ROWS, :], gain).astype(h_ref.dtype)
        q = _dot(h_ref[r0:r0 + rt, :], wq_ref[...]).astype(BF16)
        for h in range(CA_HEADS):
            hs = slice(h * HEAD_DIM, (h + 1) * HEAD_DIM)
            s = _dot_nt(q[:, hs], k_ref[:, hs]) * (SCALE * LOG2E)
            mx = jnp.max(s, axis=-1, keepdims=True)
            p = jnp.exp2(s - mx)
            den = jnp.sum(p, axis=-1, keepdims=True)
            att_ref[r0:r0 + rt, hs] = (_dot(p.astype(BF16), v_ref[:, hs]) / den).astype(BF16)
        o_ref[r0:r0 + rt, :] = x_ref[r0:r0 + rt, :] + _dot(att_ref[r0:r0 + rt, :], wo_ref[...])


def _cross(x, gain, wq, kv, wo, *, s_len, n_mem, tm=512):
    m, d = x.shape
    caw = wq.shape[1]
    tm = min(tm, s_len)
    assert m % tm == 0 and s_len % tm == 0 and caw == CA_HEADS * HEAD_DIM
    per_seq = s_len // tm
    vmem = 4 * tm * d * 4 + tm * d * 2 + 2 * 2 * d * caw * 2 + 4 * n_mem * caw * 2 + tm * caw * 2
    return pl.pallas_call(
        functools.partial(_cross_kernel, tm=tm),
        grid=(m // tm,),
        in_specs=[
            pl.BlockSpec((tm, d), lambda i: (i, 0)),
            pl.BlockSpec((1, d), lambda i: (0, 0)),
            pl.BlockSpec((d, caw), lambda i: (0, 0)),
            pl.BlockSpec((n_mem, caw), lambda i: (i // per_seq, 0)),
            pl.BlockSpec((n_mem, caw), lambda i: (i // per_seq, 1)),
            pl.BlockSpec((caw, d), lambda i: (0, 0)),
        ],
        out_specs=pl.BlockSpec((tm, d), lambda i: (i, 0)),
        out_shape=jax.ShapeDtypeStruct((m, d), F32),
        scratch_shapes=[pltpu.VMEM((tm, d), BF16), pltpu.VMEM((tm, caw), BF16)],
        compiler_params=pltpu.CompilerParams(
            dimension_semantics=("parallel",), vmem_limit_bytes=_vmem_limit(vmem)),
        name="cross",
    )(x, gain, wq, kv, kv, wo)


def _trunk(x3, mem3, p, late=None):
    bsz, s_len, d = x3.shape
    n_mem = mem3.shape[1]
    x = x3.reshape(bsz * s_len, d)
    mem = mem3.reshape(bsz * n_mem, d)
    cos_t, sin_t = _rope_tables(s_len)

    ffn1 = functools.partial(_ffn, x, p["ffn1_norm"], p["ffn1_w_gate"], p["ffn1_w_up"], p["ffn1_w_down"],
                             p["final_norm"], final=False)
    if late is None:
        x, *cast = ffn1(cast=tuple(p["late_f32"].values()))
        late = dict(zip(p["late_f32"], cast))
    else:
        x = ffn1()
    proj = _proj_in(x, p["mix_norm"], late["w_in"], cos_t, sin_t, s_len=s_len, rope_ranges=p["rope_ranges"])
    c = p["cols"]
    oa = _attn_a(proj, p["a_sink"], p["a_out_norm"], bsz=bsz, s_len=s_len,
                 q_col=c["qa"], k_col=c["ka"], v_col=c["va"], n_q=p["a_q_heads"], n_kv=p["a_kv_heads"])
    ob = _attn_b(proj, p["b_bias"], p["b_out_norm"], bsz=bsz, s_len=s_len,
                 q_col=c["qb"], k_col=c["kb"], v_col=c["vb"], n_h=p["b_heads"])
    x = _out_proj(oa, ob, late["w_out"], x)
    kv = _mem_kv(mem, p["mem_norm"], p["ca_w_kv"])
    x = _cross(x, p["ca_norm"], p["ca_w_q"], kv, p["ca_w_o"], s_len=s_len, n_mem=n_mem)
    y = _ffn(x, p["ffn2_norm"], late["ffn2_w_gate"], late["ffn2_w_up"], late["ffn2_w_down"], p["final_norm"],
             final=True)
    return y.reshape(bsz, s_len, d), late


def _prepare(ffn1_norm, ffn1_w_gate, ffn1_w_up, ffn1_w_down, mix_norm, w_in, a_sink, b_rel_bias, a_out_norm,
             b_out_norm, w_out, ca_norm, mem_norm, ca_w_q, ca_w_kv, ca_w_o, ffn2_norm, ffn2_w_gate, ffn2_w_up,
             ffn2_w_down, final_norm):
    d = w_in.shape[1]
    n_heads = d // HEAD_DIM
    a_q = n_heads // 2
    a_kv = a_q // 4
    b_h = n_heads - a_q
    aq_w, akv_w, b_w = a_q * HEAD_DIM, a_kv * HEAD_DIM, b_h * HEAD_DIM
    cols = {"qa": 0, "ka": aq_w, "va": aq_w + akv_w, "qb": aq_w + 2 * akv_w, "kb": aq_w + 2 * akv_w + b_w,
            "vb": aq_w + 2 * akv_w + 2 * b_w}

    def gain(g):
        return g.reshape(1, -1).astype(F32)

    def wt(a):
        return a[0].astype(BF16)

    return {
        "ffn1_norm": gain(ffn1_norm[0]), "ffn1_w_gate": wt(ffn1_w_gate), "ffn1_w_up": wt(ffn1_w_up),
        "ffn1_w_down": wt(ffn1_w_down),
        "mix_norm": gain(mix_norm[0]), "cols": cols,
        "rope_ranges": ((cols["qa"], cols["qa"] + aq_w), (cols["ka"], cols["ka"] + akv_w)),
        "a_sink": a_sink[0].astype(F32), "b_bias": _bias_tables(b_rel_bias[0]),
        "a_out_norm": gain(a_out_norm[0]), "b_out_norm": gain(b_out_norm[0]),
        "a_q_heads": a_q, "a_kv_heads": a_kv, "b_heads": b_h,
        "ca_norm": gain(ca_norm[0]), "mem_norm": gain(mem_norm[0]), "ca_w_q": wt(ca_w_q), "ca_w_kv": wt(ca_w_kv),
        "ca_w_o": wt(ca_w_o),
        "ffn2_norm": gain(ffn2_norm[0]), "final_norm": gain(final_norm),
        "late_f32": {"w_in": w_in[0], "w_out": w_out[0], "ffn2_w_gate": ffn2_w_gate[0], "ffn2_w_up": ffn2_w_up[0],
                     "ffn2_w_down": ffn2_w_down[0]},
    }


def kernel(x_prompt, x_sample, mem_prompt, mem_sample, ffn1_norm, ffn1_w_gate, ffn1_w_up, ffn1_w_down, mix_norm, w_in, a_sink, b_rel_bias, a_out_norm, b_out_norm, w_out, ca_norm, mem_norm, ca_w_q, ca_w_kv, ca_w_o, ffn2_norm, ffn2_w_gate, ffn2_w_up, ffn2_w_down, final_norm):
    assert ffn1_w_gate.shape[0] == 1, "single-layer trunk"
    p = _prepare(ffn1_norm, ffn1_w_gate, ffn1_w_up, ffn1_w_down, mix_norm, w_in, a_sink, b_rel_bias, a_out_norm,
                 b_out_norm, w_out, ca_norm, mem_norm, ca_w_q, ca_w_kv, ca_w_o, ffn2_norm, ffn2_w_gate, ffn2_w_up,
                 ffn2_w_down, final_norm)
    y_prompt, late = _trunk(x_prompt, mem_prompt, p)
    y_sample, _ = _trunk(x_sample, mem_sample, p, late)
    return (y_prompt, y_sample)
```

```python
import functools

import jax
import jax.numpy as jnp
from jax import lax
from jax.experimental import pallas as pl
from jax.experimental.pallas import tpu as pltpu

F32 = jnp.float32
BF16 = jnp.bfloat16

HEAD_DIM = 128
WINDOW = 128
ROPE_THETA = 500000.0
ROPE_DIM = HEAD_DIM // 4
GRID_W = 64
NA_KH = 8
NA_KW = 16
CA_HEADS = 4
EPS = 1e-6
NEG = -1e30
SCALE = HEAD_DIM ** -0.5
LOG2E = 1.4426950408889634

V7X_VMEM_BYTES = 64 * 1024 * 1024
V7X_LANES = 128
ROPE_ROW_SPLIT = 8
CROSS_ROW_SPLIT = 2
CAST_ROWS = 16
NORM_ROWS = 16
NORM_UNROLL = 4


def _vmem_limit(nbytes):
    return int(min(nbytes + 12 * 1024 * 1024, V7X_VMEM_BYTES - 4 * 1024 * 1024))


def _rms(x, gain):
    ms = jnp.mean(x * x, axis=-1, keepdims=True)
    return x * lax.rsqrt(ms + EPS) * gain


def _norm_rows(src_ref, gain_ref, dst_ref, rows):
    gain = gain_ref[...]

    def body(i, carry):
        r = pl.multiple_of(i * NORM_ROWS, NORM_ROWS)
        x = src_ref[pl.ds(r, NORM_ROWS), :]
        dst_ref[pl.ds(r, NORM_ROWS), :] = _rms(x, gain).astype(dst_ref.dtype)
        return carry

    lax.fori_loop(0, rows // NORM_ROWS, body, 0, unroll=NORM_UNROLL)


def _dot(a, b):
    return jnp.dot(a, b, preferred_element_type=F32)


def _dot_nt(a, b):
    return lax.dot_general(a, b, (((1,), (1,)), ((), ())), preferred_element_type=F32)


def _ffn_kernel(x_hbm, gin_ref, *refs, tm, nf, cps, ncol, n_cast, final):
    w_refs, refs = refs[:3 * cps], refs[3 * cps:]
    gfin_ref, cast_in, refs = refs[0], refs[1:1 + n_cast], refs[1 + n_cast:]
    o_ref, cast_out, (xbuf, h_ref, rs_ref, sem) = refs[0], refs[1:1 + n_cast], refs[1 + n_cast:]
    i = pl.program_id(0)
    s = pl.program_id(1)
    n_tiles = pl.num_programs(0)
    n_steps = pl.num_programs(1)
    d = o_ref.shape[1]

    def x_copy(tile):
        return pltpu.make_async_copy(x_hbm.at[pl.ds(pl.multiple_of(tile * tm, tm), tm), :], xbuf, sem)

    @pl.when(s == 0)
    def _():
        @pl.when(i == 0)
        def _():
            x_copy(0).start()

        x_copy(i).wait()
        gin = gin_ref[...]

        def first(j, carry):
            rows = pl.ds(pl.multiple_of(j * NORM_ROWS, NORM_ROWS), NORM_ROWS)
            x = xbuf[rows, :]
            h_ref[rows, :] = _rms(x, gin).astype(h_ref.dtype)
            o_ref[rows, :] = x
            return carry

        lax.fori_loop(0, tm // NORM_ROWS, first, 0, unroll=NORM_UNROLL)

        @pl.when(i + 1 < n_tiles)
        def _():
            x_copy(i + 1).start()

    def chunk(wg_ref, wu_ref, wd_ref, row_scales):
        h = h_ref[...]
        g = _dot(h, wg_ref[...])
        u = _dot(h, wu_ref[...])
        act = (0.5 * (g / (1.0 + jnp.exp(-g)) * u)).astype(BF16)
        sumsq = jnp.zeros((tm, 1), F32)
        for c in range(0, d, ncol):
            y = o_ref[:, c:c + ncol] + _dot(act, wd_ref[:, c:c + ncol])
            o_ref[:, c:c + ncol] = y
            if row_scales:
                sumsq = sumsq + jnp.sum(y * y, axis=-1, keepdims=True)
        if row_scales:
            rs_ref[...] = jnp.broadcast_to(lax.rsqrt(sumsq / d + EPS), rs_ref.shape)

    def chunks(count, last_step):
        for c in range(count):
            chunk(*w_refs[3 * c:3 * c + 3], final and last_step and c == count - 1)
        for src, dst in zip(cast_in, cast_out):
            dst[...] = src[...].astype(dst.dtype)

    tail = nf - (pl.cdiv(nf, cps) - 1) * cps
    pl.when(s < n_steps - 1)(functools.partial(chunks, cps, False))
    pl.when(s == n_steps - 1)(functools.partial(chunks, tail, True))

    if final:
        @pl.when(s == n_steps - 1)
        def _():
            gfin = gfin_ref[...]

            def scale(j, carry):
                rows = pl.ds(pl.multiple_of(j * NORM_ROWS, NORM_ROWS), NORM_ROWS)
                o_ref[rows, :] = o_ref[rows, :] * rs_ref[rows, :][:, :1] * gfin
                return carry

            lax.fori_loop(0, tm // NORM_ROWS, scale, 0, unroll=NORM_UNROLL)


def _cast_spec(shape, n_tiles, n_steps):
    r, c = shape
    grid_steps = n_tiles * n_steps
    assert r % CAST_ROWS == 0
    groups = r // CAST_ROWS
    per_block = min(k for k in range(1, groups + 1) if groups % k == 0 and groups // k <= grid_steps)
    row_blocks = groups // per_block
    splits = 1
    while row_blocks * splits * 2 <= grid_steps and c % (splits * 2 * V7X_LANES) == 0:
        splits *= 2
    n_blocks = row_blocks * splits

    def index(i, s):
        t = jnp.minimum(i * n_steps + s, n_blocks - 1)
        return (t // splits, t % splits)

    return pl.BlockSpec((per_block * CAST_ROWS, c // splits), index)


def _ffn(x, gin, wg, wu, wd, gfin, *, final, cast=(), tm=512, tf=256, cps=2, ncol=512):
    m, d = x.shape
    dff = wg.shape[1]
    tm = min(tm, m)
    nf = dff // tf
    cps = min(cps, nf)
    n_steps = pl.cdiv(nf, cps)
    assert m % tm == 0 and dff % tf == 0 and d % ncol == 0
    cast_specs = [_cast_spec(a.shape, m // tm, n_steps) for a in cast]

    def chunk_specs(c):
        def col(i, s):
            return (0, jnp.minimum(s * cps + c, nf - 1))

        def row(i, s):
            return (jnp.minimum(s * cps + c, nf - 1), 0)

        return [pl.BlockSpec((d, tf), col), pl.BlockSpec((d, tf), col), pl.BlockSpec((tf, d), row)]

    w_specs = [spec for c in range(cps) for spec in chunk_specs(c)]
    vmem = (tm * d * 4 + 2 * tm * d * 4 + tm * d * 2 + 2 * cps * 3 * d * tf * 2
            + sum(2 * 6 * spec.block_shape[0] * spec.block_shape[1] for spec in cast_specs))
    outs = pl.pallas_call(
        functools.partial(_ffn_kernel, tm=tm, nf=nf, cps=cps, ncol=ncol, n_cast=len(cast), final=final),
        grid=(m // tm, n_steps),
        in_specs=[
            pl.BlockSpec(memory_space=pl.ANY),
            pl.BlockSpec((1, d), lambda i, s: (0, 0)),
            *w_specs,
            pl.BlockSpec((1, d), lambda i, s: (0, 0)),
            *cast_specs,
        ],
        out_specs=[pl.BlockSpec((tm, d), lambda i, s: (i, 0)), *cast_specs],
        out_shape=[jax.ShapeDtypeStruct((m, d), F32), *[jax.ShapeDtypeStruct(a.shape, BF16) for a in cast]],
        scratch_shapes=[pltpu.VMEM((tm, d), F32), pltpu.VMEM((tm, d), BF16), pltpu.VMEM((tm, V7X_LANES), F32),
                        pltpu.SemaphoreType.DMA],
        compiler_params=pltpu.CompilerParams(
            dimension_semantics=("arbitrary", "arbitrary"), vmem_limit_bytes=_vmem_limit(vmem)),
        name="ffn_final" if final else "ffn",
    )(x, gin, *([wg, wu, wd] * cps), gfin, *cast)
    return outs[0] if not cast else tuple(outs)


def _rope_tables(s_len):
    half = ROPE_DIM // 2
    inv = 1.0 / (ROPE_THETA ** (jnp.arange(half, dtype=F32) / half))
    ang = jnp.arange(s_len).astype(F32)[:, None] * inv[None, :]
    cos, sin = jnp.cos(ang), jnp.sin(ang)
    rest = HEAD_DIM - ROPE_DIM
    cos_t = jnp.concatenate([cos, cos, jnp.ones((s_len, rest), F32)], axis=1)
    sin_t = jnp.concatenate([-sin, sin, jnp.zeros((s_len, rest), F32)], axis=1)
    return cos_t, sin_t


def _proj_kernel(x_ref, gin_ref, w_ref, cos_ref, sin_ref, o_ref, h_ref, *, tm, tn, rope_ranges):
    j = pl.program_id(1)
    col = j * tn
    is_rope = functools.reduce(jnp.logical_or, [(col >= lo) & (col < hi) for lo, hi in rope_ranges])

    def rope_step(with_norm):
        gain = gin_ref[...]
        cos = cos_ref[...]
        sin = sin_ref[...]
        half = ROPE_DIM // 2
        rt = tm // ROPE_ROW_SPLIT
        lane = lax.broadcasted_iota(jnp.int32, (rt, HEAD_DIM), 1)
        for r0 in range(0, tm, rt):
            if with_norm:
                for g0 in range(r0, r0 + rt, NORM_ROWS):
                    h_ref[g0:g0 + NORM_ROWS, :] = _rms(x_ref[g0:g0 + NORM_ROWS, :], gain).astype(h_ref.dtype)
            acc = _dot(h_ref[r0:r0 + rt, :], w_ref[...])
            for c in range(0, tn, HEAD_DIM):
                xh = acc[:, c:c + HEAD_DIM]
                partner = jnp.where(lane < half, pltpu.roll(xh, HEAD_DIM - half, 1), pltpu.roll(xh, half, 1))
                rot = xh * cos[r0:r0 + rt] + partner * sin[r0:r0 + rt]
                o_ref[r0:r0 + rt, c:c + HEAD_DIM] = jnp.where(lane < ROPE_DIM, rot, xh).astype(o_ref.dtype)

    pl.when(j == 0)(functools.partial(rope_step, True))
    pl.when(jnp.logical_and(is_rope, j > 0))(functools.partial(rope_step, False))

    @pl.when(jnp.logical_not(is_rope))
    def _():
        o_ref[...] = _dot(h_ref[...], w_ref[...]).astype(o_ref.dtype)


def _proj_in(x, gin, w, cos_t, sin_t, *, s_len, rope_ranges, tm=1024, tn=512):
    m, d = x.shape
    n = w.shape[1]
    tm = min(tm, s_len)
    assert m % tm == 0 and s_len % tm == 0 and n % tn == 0
    assert all(lo % tn == 0 and hi % tn == 0 for lo, hi in rope_ranges) and rope_ranges[0][0] == 0
    per_seq = s_len // tm
    vmem = 2 * tm * d * 4 + tm * d * 2 + 2 * d * tn * 2 + 2 * tm * tn * 2 + 4 * tm * HEAD_DIM * 4
    return pl.pallas_call(
        functools.partial(_proj_kernel, tm=tm, tn=tn, rope_ranges=rope_ranges),
        grid=(m // tm, n // tn),
        in_specs=[
            pl.BlockSpec((tm, d), lambda i, j: (i, 0)),
            pl.BlockSpec((1, d), lambda i, j: (0, 0)),
            pl.BlockSpec((d, tn), lambda i, j: (0, j)),
            pl.BlockSpec((tm, HEAD_DIM), lambda i, j: (i % per_seq, 0)),
            pl.BlockSpec((tm, HEAD_DIM), lambda i, j: (i % per_seq, 0)),
        ],
        out_specs=pl.BlockSpec((tm, tn), lambda i, j: (i, j)),
        out_shape=jax.ShapeDtypeStruct((m, n), BF16),
        scratch_shapes=[pltpu.VMEM((tm, d), BF16)],
        compiler_params=pltpu.CompilerParams(
            dimension_semantics=("parallel", "arbitrary"), vmem_limit_bytes=_vmem_limit(vmem)),
        name="proj_in",
    )(x, gin, w, cos_t, sin_t)


def _attn_a_kernel(sink_ref, q_ref, kp_ref, kc_ref, kn_ref, vp_ref, vc_ref, vn_ref, gain_ref, o_ref, oacc_ref,
                   *, nb, n_kv, group):
    n = pl.program_id(1)
    blk = WINDOW
    rows = group * blk
    rr = lax.broadcasted_iota(jnp.int32, (rows, 3 * blk), 0) & (blk - 1)
    cc = lax.broadcasted_iota(jnp.int32, (rows, 3 * blk), 1)
    lo = jnp.where(n == 0, blk, 0)
    hi = jnp.where(n == nb - 1, 2 * blk, 3 * blk)
    valid = (cc >= rr) & (cc <= rr + 2 * WINDOW) & (cc >= lo) & (cc < hi)
    grp = lax.broadcasted_iota(jnp.int32, (rows, 1), 0) // blk
    for kv in range(n_kv):
        ks = slice(kv * HEAD_DIM, (kv + 1) * HEAD_DIM)
        k = jnp.concatenate([kp_ref[:, ks], kc_ref[:, ks], kn_ref[:, ks]], axis=0)
        v = jnp.concatenate([vp_ref[:, ks], vc_ref[:, ks], vn_ref[:, ks]], axis=0)
        q = jnp.concatenate(
            [q_ref[:, (kv * group + g) * HEAD_DIM:(kv * group + g + 1) * HEAD_DIM] for g in range(group)], axis=0)
        s = _dot_nt(q, k) * (SCALE * LOG2E)
        s = jnp.where(valid, s, NEG)
        sink = jnp.zeros((rows, 1), F32)
        for g in range(group):
            sink = jnp.where(grp == g, sink_ref[kv * group + g] * LOG2E, sink)
        mx = jnp.maximum(jnp.max(s, axis=-1, keepdims=True), sink)
        p = jnp.exp2(s - mx)
        den = jnp.sum(p, axis=-1, keepdims=True) + jnp.exp2(sink - mx)
        o = _dot(p.astype(BF16), v) / den
        for g in range(group):
            h0 = (kv * group + g) * HEAD_DIM
            oacc_ref[:, h0:h0 + HEAD_DIM] = o[g * blk:(g + 1) * blk, :]
    o_ref[...] = _rms(oacc_ref[...], gain_ref[...]).astype(o_ref.dtype)


def _attn_a(proj, sink, gain, *, bsz, s_len, q_col, k_col, v_col, n_q, n_kv):
    m = proj.shape[0]
    blk = WINDOW
    nb = s_len // blk
    qw, kw = n_q * HEAD_DIM, n_kv * HEAD_DIM
    assert s_len % blk == 0 and q_col % qw == 0 and k_col % kw == 0 and v_col % kw == 0
    qb, kb, vb = q_col // qw, k_col // kw, v_col // kw

    def prev_map(col):
        return lambda b, n: (b * nb + jnp.maximum(n - 1, 0), col)

    def cur_map(col):
        return lambda b, n: (b * nb + n, col)

    def next_map(col):
        return lambda b, n: (b * nb + jnp.minimum(n + 1, nb - 1), col)

    return pl.pallas_call(
        functools.partial(_attn_a_kernel, nb=nb, n_kv=n_kv, group=n_q // n_kv),
        grid=(bsz, nb),
        in_specs=[
            pl.BlockSpec(memory_space=pltpu.SMEM),
            pl.BlockSpec((blk, qw), cur_map(qb)),
            pl.BlockSpec((blk, kw), prev_map(kb)),
            pl.BlockSpec((blk, kw), cur_map(kb)),
            pl.BlockSpec((blk, kw), next_map(kb)),
            pl.BlockSpec((blk, kw), prev_map(vb)),
            pl.BlockSpec((blk, kw), cur_map(vb)),
            pl.BlockSpec((blk, kw), next_map(vb)),
            pl.BlockSpec((1, qw), lambda b, n: (0, 0)),
        ],
        out_specs=pl.BlockSpec((blk, qw), lambda b, n: (b * nb + n, 0)),
        out_shape=jax.ShapeDtypeStruct((m, qw), BF16),
        scratch_shapes=[pltpu.VMEM((blk, qw), F32)],
        compiler_params=pltpu.CompilerParams(dimension_semantics=("parallel", "arbitrary")),
        name="attn_a",
    )(sink, proj, proj, proj, proj, proj, proj, proj, gain)


def _bias_tables(rel_bias):
    n_h, n_dr, _ = rel_bias.shape
    col = jnp.arange(GRID_W)
    col_start = jnp.clip(col - NA_KW // 2, 0, GRID_W - NA_KW)
    inside = (col[None, :] >= col_start[:, None]) & (col[None, :] < col_start[:, None] + NA_KW)
    dc = jnp.clip(col[None, :] - col[:, None] + (NA_KW - 1), 0, 2 * NA_KW - 2)
    tc = jnp.where(inside[None, None], rel_bias.astype(F32)[:, :, dc], NEG)
    pairs = jnp.concatenate([tc[:, :-1], tc[:, 1:]], axis=-1)
    return pairs.reshape(n_h * (n_dr - 1), GRID_W, 2 * GRID_W) * LOG2E


def _attn_b_kernel(q_ref, k_ref, v_ref, bias_ref, gain_ref, o_ref, orow_ref, s_ref, p_ref, *, rows, n_h, rpb):
    rb = pl.program_id(1)
    band0 = jnp.clip(rpb * rb - NA_KH // 2, 0, rows - 2 * rpb)
    n_pair = 2 * NA_KH - 2
    gain = gain_ref[...]

    def row_body(ri, carry):
        r = rpb * rb + ri
        r0 = jnp.clip(r - NA_KH // 2, 0, rows - NA_KH)
        off = pl.multiple_of((r0 - band0) * GRID_W, GRID_W)
        qoff = pl.multiple_of(ri * GRID_W, GRID_W)
        dr0 = r0 - r + (NA_KH - 1)
        for h in range(n_h):
            hs = slice(h * HEAD_DIM, (h + 1) * HEAD_DIM)
            q = q_ref[pl.ds(qoff, GRID_W), hs]
            kk = k_ref[pl.ds(off, NA_KH * GRID_W), hs]
            bias = jnp.concatenate([bias_ref[h * n_pair + dr0 + 2 * t] for t in range(NA_KH // 2)], axis=1)
            s_ref[h * GRID_W:(h + 1) * GRID_W, :] = _dot_nt(q, kk) * (SCALE * LOG2E) + bias
        s = s_ref[...]
        mx = jnp.max(s, axis=-1, keepdims=True)
        p = jnp.exp2(s - mx)
        den = jnp.sum(p, axis=-1, keepdims=True)
        p_ref[...] = p.astype(BF16)
        for h in range(n_h):
            hs = slice(h * HEAD_DIM, (h + 1) * HEAD_DIM)
            rs = slice(h * GRID_W, (h + 1) * GRID_W)
            vv = v_ref[pl.ds(off, NA_KH * GRID_W), hs]
            orow_ref[:, hs] = _dot(p_ref[rs, :], vv) / den[rs]
        o_ref[pl.ds(qoff, GRID_W), :] = _rms(orow_ref[...], gain).astype(o_ref.dtype)
        return carry

    lax.fori_loop(0, rpb, row_body, 0)


def _attn_b(proj, bias_t, gain, *, bsz, s_len, q_col, k_col, v_col, n_h, rpb=8):
    m = proj.shape[0]
    rows = s_len // GRID_W
    width = n_h * HEAD_DIM
    assert rows % rpb == 0 and rows >= 2 * rpb and rpb >= NA_KH
    assert all(c % V7X_LANES == 0 for c in (q_col, k_col, v_col))
    nrb = rows // rpb
    band = 2 * rpb * GRID_W
    tq = rpb * GRID_W

    def band_map(col):
        def index(b, rb):
            start = jnp.clip(rpb * rb - NA_KH // 2, 0, rows - 2 * rpb)
            return (pl.multiple_of((b * rows + start) * GRID_W, GRID_W), col)
        return index

    keys = NA_KH * GRID_W
    vmem = (2 * (tq * width * 2 + 2 * band * width * 2 + tq * width * 2) + 2 * bias_t.size * 4 + GRID_W * width * 4
            + n_h * GRID_W * keys * 6)
    return pl.pallas_call(
        functools.partial(_attn_b_kernel, rows=rows, n_h=n_h, rpb=rpb),
        grid=(bsz, nrb),
        in_specs=[
            pl.BlockSpec((pl.Element(tq), pl.Element(width)),
                         lambda b, rb: (pl.multiple_of((b * nrb + rb) * tq, tq), q_col)),
            pl.BlockSpec((pl.Element(band), pl.Element(width)), band_map(k_col)),
            pl.BlockSpec((pl.Element(band), pl.Element(width)), band_map(v_col)),
            pl.BlockSpec(bias_t.shape, lambda b, rb: (0, 0, 0)),
            pl.BlockSpec((1, width), lambda b, rb: (0, 0)),
        ],
        out_specs=pl.BlockSpec((tq, width), lambda b, rb: (b * nrb + rb, 0)),
        out_shape=jax.ShapeDtypeStruct((m, width), BF16),
        scratch_shapes=[pltpu.VMEM((GRID_W, width), F32), pltpu.VMEM((n_h * GRID_W, keys), F32),
                        pltpu.VMEM((n_h * GRID_W, keys), BF16)],
        compiler_params=pltpu.CompilerParams(
            dimension_semantics=("parallel", "arbitrary"), vmem_limit_bytes=_vmem_limit(vmem)),
        name="attn_b",
    )(proj, proj, proj, bias_t, gain)


def _out_proj_kernel(a_ref, b_ref, wa_ref, wb_ref, x_ref, o_ref):
    o_ref[...] = x_ref[...] + (_dot(a_ref[...], wa_ref[...]) + _dot(b_ref[...], wb_ref[...]))


def _out_proj(oa, ob, w, x, *, tm=1024, tn=1024):
    m, d = x.shape
    ka, kb = oa.shape[1], ob.shape[1]
    tm = min(tm, m)
    assert ka == kb and m % tm == 0 and d % tn == 0
    vmem = 2 * (2 * tm * ka * 2 + 2 * ka * tn * 2 + 2 * tm * tn * 4)
    return pl.pallas_call(
        _out_proj_kernel,
        grid=(m // tm, d // tn),
        in_specs=[
            pl.BlockSpec((tm, ka), lambda i, j: (i, 0)),
            pl.BlockSpec((tm, kb), lambda i, j: (i, 0)),
            pl.BlockSpec((ka, tn), lambda i, j: (0, j)),
            pl.BlockSpec((kb, tn), lambda i, j: (1, j)),
            pl.BlockSpec((tm, tn), lambda i, j: (i, j)),
        ],
        out_specs=pl.BlockSpec((tm, tn), lambda i, j: (i, j)),
        out_shape=jax.ShapeDtypeStruct((m, d), F32),
        compiler_params=pltpu.CompilerParams(
            dimension_semantics=("parallel", "arbitrary"), vmem_limit_bytes=_vmem_limit(vmem)),
        name="out_proj",
    )(oa, ob, w, w, x)


def _mem_kv_kernel(mem_ref, gain_ref, w_ref, o_ref, h_ref, *, tm):
    _norm_rows(mem_ref, gain_ref, h_ref, tm)
    o_ref[...] = _dot(h_ref[...], w_ref[...]).astype(o_ref.dtype)


def _mem_kv(mem, gain, w, *, tm=256):
    m, d = mem.shape
    n = w.shape[1]
    tm = min(tm, m)
    assert m % tm == 0
    vmem = 2 * tm * d * 4 + tm * d * 2 + 2 * d * n * 2 + 2 * tm * n * 2
    return pl.pallas_call(
        functools.partial(_mem_kv_kernel, tm=tm),
        grid=(m // tm,),
        in_specs=[
            pl.BlockSpec((tm, d), lambda i: (i, 0)),
            pl.BlockSpec((1, d), lambda i: (0, 0)),
            pl.BlockSpec((d, n), lambda i: (0, 0)),
        ],
        out_specs=pl.BlockSpec((tm, n), lambda i: (i, 0)),
        out_shape=jax.ShapeDtypeStruct((m, n), BF16),
        scratch_shapes=[pltpu.VMEM((tm, d), BF16)],
        compiler_params=pltpu.CompilerParams(
            dimension_semantics=("parallel",), vmem_limit_bytes=_vmem_limit(vmem)),
        name="mem_kv",
    )(mem, gain, w)


def _cross_kernel(x_ref, gain_ref, wq_ref, k_ref, v_ref, wo_ref, o_ref, h_ref, att_ref, *, tm):
    gain = gain_ref[...]
    rt = tm // CROSS_ROW_SPLIT
    for r0 in range(0, tm, rt):
        for g0 in range(r0, r0 + rt, NORM_ROWS):
            h_ref[g0:g0 + NORM_ROWS, :] = _rms(x_ref[g0:g0 + NORM_ROWS, :], gain).astype(h_ref.dtype)
        q = _dot(h_ref[r0:r0 + rt, :], wq_ref[...]).astype(BF16)
        for h in range(CA_HEADS):
            hs = slice(h * HEAD_DIM, (h + 1) * HEAD_DIM)
            s = _dot_nt(q[:, hs], k_ref[:, hs]) * (SCALE * LOG2E)
            mx = jnp.max(s, axis=-1, keepdims=True)
            p = jnp.exp2(s - mx)
            den = jnp.sum(p, axis=-1, keepdims=True)
            att_ref[r0:r0 + rt, hs] = (_dot(p.astype(BF16), v_ref[:, hs]) / den).astype(BF16)
        o_ref[r0:r0 + rt, :] = x_ref[r0:r0 + rt, :] + _dot(att_ref[r0:r0 + rt, :], wo_ref[...])


def _cross(x, gain, wq, kv, wo, *, s_len, n_mem, tm=512):
    m, d = x.shape
    caw = wq.shape[1]
    tm = min(tm, s_len)
    assert m % tm == 0 and s_len % tm == 0 and caw == CA_HEADS * HEAD_DIM
    per_seq = s_len // tm
    vmem = 4 * tm * d * 4 + tm * d * 2 + 2 * 2 * d * caw * 2 + 4 * n_mem * caw * 2 + tm * caw * 2
    return pl.pallas_call(
        functools.partial(_cross_kernel, tm=tm),
        grid=(m // tm,),
        in_specs=[
            pl.BlockSpec((tm, d), lambda i: (i, 0)),
            pl.BlockSpec((1, d), lambda i: (0, 0)),
            pl.BlockSpec((d, caw), lambda i: (0, 0)),
            pl.BlockSpec((n_mem, caw), lambda i: (i // per_seq, 0)),
            pl.BlockSpec((n_mem, caw), lambda i: (i // per_seq, 1)),
            pl.BlockSpec((caw, d), lambda i: (0, 0)),
        ],
        out_specs=pl.BlockSpec((tm, d), lambda i: (i, 0)),
        out_shape=jax.ShapeDtypeStruct((m, d), F32),
        scratch_shapes=[pltpu.VMEM((tm, d), BF16), pltpu.VMEM((tm, caw), BF16)],
        compiler_params=pltpu.CompilerParams(
            dimension_semantics=("parallel",), vmem_limit_bytes=_vmem_limit(vmem)),
        name="cross",
    )(x, gain, wq, kv, kv, wo)


def _trunk(x3, mem3, p, late=None):
    bsz, s_len, d = x3.shape
    n_mem = mem3.shape[1]
    x = x3.reshape(bsz * s_len, d)
    mem = mem3.reshape(bsz * n_mem, d)
    cos_t, sin_t = _rope_tables(s_len)

    ffn1 = functools.partial(_ffn, x, p["ffn1_norm"], p["ffn1_w_gate"], p["ffn1_w_up"], p["ffn1_w_down"],
                             p["final_norm"], final=False)
    if late is None:
        x, *cast = ffn1(cast=tuple(p["late_f32"].values()))
        late = dict(zip(p["late_f32"], cast))
    else:
        x = ffn1()
    proj = _proj_in(x, p["mix_norm"], late["w_in"], cos_t, sin_t, s_len=s_len, rope_ranges=p["rope_ranges"])
    c = p["cols"]
    oa = _attn_a(proj, p["a_sink"], p["a_out_norm"], bsz=bsz, s_len=s_len,
                 q_col=c["qa"], k_col=c["ka"], v_col=c["va"], n_q=p["a_q_heads"], n_kv=p["a_kv_heads"])
    ob = _attn_b(proj, p["b_bias"], p["b_out_norm"], bsz=bsz, s_len=s_len,
                 q_col=c["qb"], k_col=c["kb"], v_col=c["vb"], n_h=p["b_heads"])
    x = _out_proj(oa, ob, late["w_out"], x)
    kv = _mem_kv(mem, p["mem_norm"], p["ca_w_kv"])
    x = _cross(x, p["ca_norm"], p["ca_w_q"], kv, p["ca_w_o"], s_len=s_len, n_mem=n_mem)
    y = _ffn(x, p["ffn2_norm"], late["ffn2_w_gate"], late["ffn2_w_up"], late["ffn2_w_down"], p["final_norm"],
             final=True)
    return y.reshape(bsz, s_len, d), late


def _prepare(ffn1_norm, ffn1_w_gate, ffn1_w_up, ffn1_w_down, mix_norm, w_in, a_sink, b_rel_bias, a_out_norm,
             b_out_norm, w_out, ca_norm, mem_norm, ca_w_q, ca_w_kv, ca_w_o, ffn2_norm, ffn2_w_gate, ffn2_w_up,
             ffn2_w_down, final_norm):
    d = w_in.shape[1]
    n_heads = d // HEAD_DIM
    a_q = n_heads // 2
    a_kv = a_q // 4
    b_h = n_heads - a_q
    aq_w, akv_w, b_w = a_q * HEAD_DIM, a_kv * HEAD_DIM, b_h * HEAD_DIM
    cols = {"qa": 0, "ka": aq_w, "va": aq_w + akv_w, "qb": aq_w + 2 * akv_w, "kb": aq_w + 2 * akv_w + b_w,
            "vb": aq_w + 2 * akv_w + 2 * b_w}

    def gain(g):
        return g.reshape(1, -1).astype(F32)

    def wt(a):
        return a[0].astype(BF16)

    return {
        "ffn1_norm": gain(ffn1_norm[0]), "ffn1_w_gate": wt(ffn1_w_gate), "ffn1_w_up": wt(ffn1_w_up),
        "ffn1_w_down": wt(ffn1_w_down),
        "mix_norm": gain(mix_norm[0]), "cols": cols,
        "rope_ranges": ((cols["qa"], cols["qa"] + aq_w), (cols["ka"], cols["ka"] + akv_w)),
        "a_sink": a_sink[0].astype(F32), "b_bias": _bias_tables(b_rel_bias[0]),
        "a_out_norm": gain(a_out_norm[0]), "b_out_norm": gain(b_out_norm[0]),
        "a_q_heads": a_q, "a_kv_heads": a_kv, "b_heads": b_h,
        "ca_norm": gain(ca_norm[0]), "mem_norm": gain(mem_norm[0]), "ca_w_q": wt(ca_w_q), "ca_w_kv": wt(ca_w_kv),
        "ca_w_o": wt(ca_w_o),
        "ffn2_norm": gain(ffn2_norm[0]), "final_norm": gain(final_norm),
        "late_f32": {"w_in": w_in[0], "w_out": w_out[0], "ffn2_w_gate": ffn2_w_gate[0], "ffn2_w_up": ffn2_w_up[0],
                     "ffn2_w_down": ffn2_w_down[0]},
    }


def kernel(x_prompt, x_sample, mem_prompt, mem_sample, ffn1_norm, ffn1_w_gate, ffn1_w_up, ffn1_w_down, mix_norm, w_in, a_sink, b_rel_bias, a_out_norm, b_out_norm, w_out, ca_norm, mem_norm, ca_w_q, ca_w_kv, ca_w_o, ffn2_norm, ffn2_w_gate, ffn2_w_up, ffn2_w_down, final_norm):
    assert ffn1_w_gate.shape[0] == 1, "single-layer trunk"
    p = _prepare(ffn1_norm, ffn1_w_gate, ffn1_w_up, ffn1_w_down, mix_norm, w_in, a_sink, b_rel_bias, a_out_norm,
                 b_out_norm, w_out, ca_norm, mem_norm, ca_w_q, ca_w_kv, ca_w_o, ffn2_norm, ffn2_w_gate, ffn2_w_up,
                 ffn2_w_down, final_norm)
    y_prompt, late = _trunk(x_prompt, mem_prompt, p)
    y_sample, _ = _trunk(x_sample, mem_sample, p, late)
    return (y_prompt, y_sample)
```

```python
import functools

import jax
import jax.numpy as jnp
from jax import lax
from jax.experimental import pallas as pl
from jax.experimental.pallas import tpu as pltpu

F32 = jnp.float32
BF16 = jnp.bfloat16

HEAD_DIM = 128
WINDOW = 128
ROPE_THETA = 500000.0
ROPE_DIM = HEAD_DIM // 4
GRID_W = 64
NA_KH = 8
NA_KW = 16
CA_HEADS = 4
EPS = 1e-6
NEG = -1e30
SCALE = HEAD_DIM ** -0.5
LOG2E = 1.4426950408889634

V7X_VMEM_BYTES = 64 * 1024 * 1024
V7X_LANES = 128
ROPE_ROW_SPLIT = 4
CROSS_ROW_SPLIT = 2
CAST_ROWS = 16
NORM_ROWS = 16
NORM_UNROLL = 4


def _vmem_limit(nbytes):
    return int(min(nbytes + 12 * 1024 * 1024, V7X_VMEM_BYTES - 4 * 1024 * 1024))


def _rms(x, gain):
    ms = jnp.mean(x * x, axis=-1, keepdims=True)
    return x * lax.rsqrt(ms + EPS) * gain


def _norm_rows(src_ref, gain_ref, dst_ref, rows):
    gain = gain_ref[...]

    def body(i, carry):
        r = pl.multiple_of(i * NORM_ROWS, NORM_ROWS)
        x = src_ref[pl.ds(r, NORM_ROWS), :]
        dst_ref[pl.ds(r, NORM_ROWS), :] = _rms(x, gain).astype(dst_ref.dtype)
        return carry

    lax.fori_loop(0, rows // NORM_ROWS, body, 0, unroll=NORM_UNROLL)


def _dot(a, b):
    return jnp.dot(a, b, preferred_element_type=F32)


def _dot_nt(a, b):
    return lax.dot_general(a, b, (((1,), (1,)), ((), ())), preferred_element_type=F32)


def _ffn_kernel(x_hbm, gin_ref, *refs, tm, nf, cps, ncol, n_cast, final):
    w_refs, refs = refs[:3 * cps], refs[3 * cps:]
    gfin_ref, cast_in, refs = refs[0], refs[1:1 + n_cast], refs[1 + n_cast:]
    o_ref, cast_out, (xbuf, h_ref, rs_ref, sem) = refs[0], refs[1:1 + n_cast], refs[1 + n_cast:]
    i = pl.program_id(0)
    s = pl.program_id(1)
    n_tiles = pl.num_programs(0)
    n_steps = pl.num_programs(1)
    d = o_ref.shape[1]

    def x_copy(tile):
        return pltpu.make_async_copy(x_hbm.at[pl.ds(pl.multiple_of(tile * tm, tm), tm), :], xbuf, sem)

    def chunk(wg_ref, wu_ref, wd_ref, row_scales, onto_x=False):
        h = h_ref[...]
        g = _dot(h, wg_ref[...])
        u = _dot(h, wu_ref[...])
        act = (0.5 * (g / (1.0 + jnp.exp(-g)) * u)).astype(BF16)
        sumsq = jnp.zeros((tm, 1), F32)
        for c in range(0, d, ncol):
            acc_ref = xbuf if onto_x else o_ref
            y = acc_ref[:, c:c + ncol] + _dot(act, wd_ref[:, c:c + ncol])
            o_ref[:, c:c + ncol] = y
            if row_scales:
                sumsq = sumsq + jnp.sum(y * y, axis=-1, keepdims=True)
        if row_scales:
            rs_ref[...] = jnp.broadcast_to(lax.rsqrt(sumsq / d + EPS), rs_ref.shape)

    def chunks(count, last_step, first_step=False):
        for c in range(count):
            chunk(*w_refs[3 * c:3 * c + 3], final and last_step and c == count - 1, first_step and c == 0)
        for src, dst in zip(cast_in, cast_out):
            dst[...] = src[...].astype(dst.dtype)

    tail = nf - (pl.cdiv(nf, cps) - 1) * cps

    @pl.when(s == 0)
    def _():
        @pl.when(i == 0)
        def _():
            x_copy(0).start()

        x_copy(i).wait()
        _norm_rows(xbuf, gin_ref, h_ref, tm)
        chunks(cps, False, first_step=True)

        @pl.when(i + 1 < n_tiles)
        def _():
            x_copy(i + 1).start()

    pl.when(jnp.logical_and(s > 0, s < n_steps - 1))(functools.partial(chunks, cps, False))
    pl.when(s == n_steps - 1)(functools.partial(chunks, tail, True))

    if final:
        @pl.when(s == n_steps - 1)
        def _():
            gfin = gfin_ref[...]

            def scale(j, carry):
                rows = pl.ds(pl.multiple_of(j * NORM_ROWS, NORM_ROWS), NORM_ROWS)
                o_ref[rows, :] = o_ref[rows, :] * rs_ref[rows, :][:, :1] * gfin
                return carry

            lax.fori_loop(0, tm // NORM_ROWS, scale, 0, unroll=NORM_UNROLL)


def _cast_spec(shape, n_tiles, n_steps):
    r, c = shape
    grid_steps = n_tiles * n_steps
    assert r % CAST_ROWS == 0
    groups = r // CAST_ROWS
    per_block = min(k for k in range(1, groups + 1) if groups % k == 0 and groups // k <= grid_steps)
    row_blocks = groups // per_block
    splits = 1
    while row_blocks * splits * 2 <= grid_steps and c % (splits * 2 * V7X_LANES) == 0:
        splits *= 2
    n_blocks = row_blocks * splits

    def index(i, s):
        t = jnp.minimum(i * n_steps + s, n_blocks - 1)
        return (t // splits, t % splits)

    return pl.BlockSpec((per_block * CAST_ROWS, c // splits), index)


def _ffn(x, gin, wg, wu, wd, gfin, *, final, cast=(), tm=512, tf=256, cps=2, ncol=512):
    m, d = x.shape
    dff = wg.shape[1]
    tm = min(tm, m)
    nf = dff // tf
    cps = min(cps, nf)
    n_steps = pl.cdiv(nf, cps)
    assert m % tm == 0 and dff % tf == 0 and d % ncol == 0 and n_steps >= 2
    cast_specs = [_cast_spec(a.shape, m // tm, n_steps) for a in cast]

    def chunk_specs(c):
        def col(i, s):
            return (0, jnp.minimum(s * cps + c, nf - 1))

        def row(i, s):
            return (jnp.minimum(s * cps + c, nf - 1), 0)

        return [pl.BlockSpec((d, tf), col), pl.BlockSpec((d, tf), col), pl.BlockSpec((tf, d), row)]

    w_specs = [spec for c in range(cps) for spec in chunk_specs(c)]
    vmem = (tm * d * 4 + 2 * tm * d * 4 + tm * d * 2 + 2 * cps * 3 * d * tf * 2
            + sum(2 * 6 * spec.block_shape[0] * spec.block_shape[1] for spec in cast_specs))
    outs = pl.pallas_call(
        functools.partial(_ffn_kernel, tm=tm, nf=nf, cps=cps, ncol=ncol, n_cast=len(cast), final=final),
        grid=(m // tm, n_steps),
        in_specs=[
            pl.BlockSpec(memory_space=pl.ANY),
            pl.BlockSpec((1, d), lambda i, s: (0, 0)),
            *w_specs,
            pl.BlockSpec((1, d), lambda i, s: (0, 0)),
            *cast_specs,
        ],
        out_specs=[pl.BlockSpec((tm, d), lambda i, s: (i, 0)), *cast_specs],
        out_shape=[jax.ShapeDtypeStruct((m, d), F32), *[jax.ShapeDtypeStruct(a.shape, BF16) for a in cast]],
        scratch_shapes=[pltpu.VMEM((tm, d), F32), pltpu.VMEM((tm, d), BF16), pltpu.VMEM((tm, V7X_LANES), F32),
                        pltpu.SemaphoreType.DMA],
        compiler_params=pltpu.CompilerParams(
            dimension_semantics=("arbitrary", "arbitrary"), vmem_limit_bytes=_vmem_limit(vmem)),
        name="ffn_final" if final else "ffn",
    )(x, gin, *([wg, wu, wd] * cps), gfin, *cast)
    return outs[0] if not cast else tuple(outs)


def _rope_tables(s_len):
    half = ROPE_DIM // 2
    inv = 1.0 / (ROPE_THETA ** (jnp.arange(half, dtype=F32) / half))
    ang = jnp.arange(s_len).astype(F32)[:, None] * inv[None, :]
    cos, sin = jnp.cos(ang), jnp.sin(ang)
    rest = HEAD_DIM - ROPE_DIM
    cos_t = jnp.concatenate([cos, cos, jnp.ones((s_len, rest), F32)], axis=1)
    sin_t = jnp.concatenate([-sin, sin, jnp.zeros((s_len, rest), F32)], axis=1)
    return cos_t, sin_t


def _proj_kernel(x_ref, gin_ref, w_ref, cos_ref, sin_ref, o_ref, h_ref, *, tm, tn, rope_ranges):
    j = pl.program_id(1)
    col = j * tn
    is_rope = functools.reduce(jnp.logical_or, [(col >= lo) & (col < hi) for lo, hi in rope_ranges])

    def rope_step(with_norm):
        gain = gin_ref[...]
        cos = cos_ref[...]
        sin = sin_ref[...]
        half = ROPE_DIM // 2
        rt = tm // ROPE_ROW_SPLIT
        lane = lax.broadcasted_iota(jnp.int32, (rt, HEAD_DIM), 1)
        for r0 in range(0, tm, rt):
            if with_norm:
                for g0 in range(r0, r0 + rt, NORM_ROWS):
                    h_ref[g0:g0 + NORM_ROWS, :] = _rms(x_ref[g0:g0 + NORM_ROWS, :], gain).astype(h_ref.dtype)
            acc = _dot(h_ref[r0:r0 + rt, :], w_ref[...])
            for c in range(0, tn, HEAD_DIM):
                xh = acc[:, c:c + HEAD_DIM]
                partner = jnp.where(lane < half, pltpu.roll(xh, HEAD_DIM - half, 1), pltpu.roll(xh, half, 1))
                rot = xh * cos[r0:r0 + rt] + partner * sin[r0:r0 + rt]
                o_ref[r0:r0 + rt, c:c + HEAD_DIM] = jnp.where(lane < ROPE_DIM, rot, xh).astype(o_ref.dtype)

    pl.when(j == 0)(functools.partial(rope_step, True))
    pl.when(jnp.logical_and(is_rope, j > 0))(functools.partial(rope_step, False))

    @pl.when(jnp.logical_not(is_rope))
    def _():
        o_ref[...] = _dot(h_ref[...], w_ref[...]).astype(o_ref.dtype)


def _proj_in(x, gin, w, cos_t, sin_t, *, s_len, rope_ranges, tm=1024, tn=512):
    m, d = x.shape
    n = w.shape[1]
    tm = min(tm, s_len)
    assert m % tm == 0 and s_len % tm == 0 and n % tn == 0
    assert all(lo % tn == 0 and hi % tn == 0 for lo, hi in rope_ranges) and rope_ranges[0][0] == 0
    per_seq = s_len // tm
    vmem = 2 * tm * d * 4 + tm * d * 2 + 2 * d * tn * 2 + 2 * tm * tn * 2 + 4 * tm * HEAD_DIM * 4
    return pl.pallas_call(
        functools.partial(_proj_kernel, tm=tm, tn=tn, rope_ranges=rope_ranges),
        grid=(m // tm, n // tn),
        in_specs=[
            pl.BlockSpec((tm, d), lambda i, j: (i, 0)),
            pl.BlockSpec((1, d), lambda i, j: (0, 0)),
            pl.BlockSpec((d, tn), lambda i, j: (0, j)),
            pl.BlockSpec((tm, HEAD_DIM), lambda i, j: (i % per_seq, 0)),
            pl.BlockSpec((tm, HEAD_DIM), lambda i, j: (i % per_seq, 0)),
        ],
        out_specs=pl.BlockSpec((tm, tn), lambda i, j: (i, j)),
        out_shape=jax.ShapeDtypeStruct((m, n), BF16),
        scratch_shapes=[pltpu.VMEM((tm, d), BF16)],
        compiler_params=pltpu.CompilerParams(
            dimension_semantics=("parallel", "arbitrary"), vmem_limit_bytes=_vmem_limit(vmem)),
        name="proj_in",
    )(x, gin, w, cos_t, sin_t)


def _attn_a_kernel(sink_ref, q_ref, kp_ref, kc_ref, kn_ref, vp_ref, vc_ref, vn_ref, gain_ref, o_ref, oacc_ref,
                   *, nb, n_kv, group):
    n = pl.program_id(1)
    blk = WINDOW
    rows = group * blk
    rr = lax.broadcasted_iota(jnp.int32, (rows, 3 * blk), 0) & (blk - 1)
    cc = lax.broadcasted_iota(jnp.int32, (rows, 3 * blk), 1)
    lo = jnp.where(n == 0, blk, 0)
    hi = jnp.where(n == nb - 1, 2 * blk, 3 * blk)
    valid = (cc >= rr) & (cc <= rr + 2 * WINDOW) & (cc >= lo) & (cc < hi)
    grp = lax.broadcasted_iota(jnp.int32, (rows, 1), 0) // blk
    for kv in range(n_kv):
        ks = slice(kv * HEAD_DIM, (kv + 1) * HEAD_DIM)
        k = jnp.concatenate([kp_ref[:, ks], kc_ref[:, ks], kn_ref[:, ks]], axis=0)
        v = jnp.concatenate([vp_ref[:, ks], vc_ref[:, ks], vn_ref[:, ks]], axis=0)
        q = jnp.concatenate(
            [q_ref[:, (kv * group + g) * HEAD_DIM:(kv * group + g + 1) * HEAD_DIM] for g in range(group)], axis=0)
        s = _dot_nt(q, k) * (SCALE * LOG2E)
        s = jnp.where(valid, s, NEG)
        sink = jnp.zeros((rows, 1), F32)
        for g in range(group):
            sink = jnp.where(grp == g, sink_ref[kv * group + g] * LOG2E, sink)
        mx = jnp.maximum(jnp.max(s, axis=-1, keepdims=True), sink)
        p = jnp.exp2(s - mx)
        den = jnp.sum(p, axis=-1, keepdims=True) + jnp.exp2(sink - mx)
        o = _dot(p.astype(BF16), v) / den
        for g in range(group):
            h0 = (kv * group + g) * HEAD_DIM
            oacc_ref[:, h0:h0 + HEAD_DIM] = o[g * blk:(g + 1) * blk, :]
    o_ref[...] = _rms(oacc_ref[...], gain_ref[...]).astype(o_ref.dtype)


def _attn_a(proj, sink, gain, *, bsz, s_len, q_col, k_col, v_col, n_q, n_kv):
    m = proj.shape[0]
    blk = WINDOW
    nb = s_len // blk
    qw, kw = n_q * HEAD_DIM, n_kv * HEAD_DIM
    assert s_len % blk == 0 and q_col % qw == 0 and k_col % kw == 0 and v_col % kw == 0
    qb, kb, vb = q_col // qw, k_col // kw, v_col // kw

    def prev_map(col):
        return lambda b, n: (b * nb + jnp.maximum(n - 1, 0), col)

    def cur_map(col):
        return lambda b, n: (b * nb + n, col)

    def next_map(col):
        return lambda b, n: (b * nb + jnp.minimum(n + 1, nb - 1), col)

    return pl.pallas_call(
        functools.partial(_attn_a_kernel, nb=nb, n_kv=n_kv, group=n_q // n_kv),
        grid=(bsz, nb),
        in_specs=[
            pl.BlockSpec(memory_space=pltpu.SMEM),
            pl.BlockSpec((blk, qw), cur_map(qb)),
            pl.BlockSpec((blk, kw), prev_map(kb)),
            pl.BlockSpec((blk, kw), cur_map(kb)),
            pl.BlockSpec((blk, kw), next_map(kb)),
            pl.BlockSpec((blk, kw), prev_map(vb)),
            pl.BlockSpec((blk, kw), cur_map(vb)),
            pl.BlockSpec((blk, kw), next_map(vb)),
            pl.BlockSpec((1, qw), lambda b, n: (0, 0)),
        ],
        out_specs=pl.BlockSpec((blk, qw), lambda b, n: (b * nb + n, 0)),
        out_shape=jax.ShapeDtypeStruct((m, qw), BF16),
        scratch_shapes=[pltpu.VMEM((blk, qw), F32)],
        compiler_params=pltpu.CompilerParams(dimension_semantics=("parallel", "arbitrary")),
        name="attn_a",
    )(sink, proj, proj, proj, proj, proj, proj, proj, gain)


def _bias_tables(rel_bias):
    n_h, n_dr, _ = rel_bias.shape
    col = jnp.arange(GRID_W)
    col_start = jnp.clip(col - NA_KW // 2, 0, GRID_W - NA_KW)
    inside = (col[None, :] >= col_start[:, None]) & (col[None, :] < col_start[:, None] + NA_KW)
    dc = jnp.clip(col[None, :] - col[:, None] + (NA_KW - 1), 0, 2 * NA_KW - 2)
    tc = jnp.where(inside[None, None], rel_bias.astype(F32)[:, :, dc], NEG)
    pairs = jnp.concatenate([tc[:, :-1], tc[:, 1:]], axis=-1)
    return pairs.reshape(n_h * (n_dr - 1), GRID_W, 2 * GRID_W) * LOG2E


def _attn_b_kernel(q_ref, k_ref, v_ref, bias_ref, gain_ref, o_ref, orow_ref, s_ref, p_ref, *, rows, n_h, rpb):
    rb = pl.program_id(1)
    band0 = jnp.clip(rpb * rb - NA_KH // 2, 0, rows - 2 * rpb)
    n_pair = 2 * NA_KH - 2
    gain = gain_ref[...]

    def row_body(ri, carry):
        r = rpb * rb + ri
        r0 = jnp.clip(r - NA_KH // 2, 0, rows - NA_KH)
        off = pl.multiple_of((r0 - band0) * GRID_W, GRID_W)
        qoff = pl.multiple_of(ri * GRID_W, GRID_W)
        dr0 = r0 - r + (NA_KH - 1)
        for h in range(n_h):
            hs = slice(h * HEAD_DIM, (h + 1) * HEAD_DIM)
            q = q_ref[pl.ds(qoff, GRID_W), hs]
            kk = k_ref[pl.ds(off, NA_KH * GRID_W), hs]
            bias = jnp.concatenate([bias_ref[h * n_pair + dr0 + 2 * t] for t in range(NA_KH // 2)], axis=1)
            s_ref[h * GRID_W:(h + 1) * GRID_W, :] = _dot_nt(q, kk) * (SCALE * LOG2E) + bias
        s = s_ref[...]
        mx = jnp.max(s, axis=-1, keepdims=True)
        p = jnp.exp2(s - mx)
        den = jnp.sum(p, axis=-1, keepdims=True)
        p_ref[...] = p.astype(BF16)
        for h in range(n_h):
            hs = slice(h * HEAD_DIM, (h + 1) * HEAD_DIM)
            rs = slice(h * GRID_W, (h + 1) * GRID_W)
            vv = v_ref[pl.ds(off, NA_KH * GRID_W), hs]
            orow_ref[:, hs] = _dot(p_ref[rs, :], vv) / den[rs]
        o_ref[pl.ds(qoff, GRID_W), :] = _rms(orow_ref[...], gain).astype(o_ref.dtype)
        return carry

    lax.fori_loop(0, rpb, row_body, 0)


def _attn_b(proj, bias_t, gain, *, bsz, s_len, q_col, k_col, v_col, n_h, rpb=8):
    m = proj.shape[0]
    rows = s_len // GRID_W
    width = n_h * HEAD_DIM
    assert rows % rpb == 0 and rows >= 2 * rpb and rpb >= NA_KH
    assert all(c % V7X_LANES == 0 for c in (q_col, k_col, v_col))
    nrb = rows // rpb
    band = 2 * rpb * GRID_W
    tq = rpb * GRID_W

    def band_map(col):
        def index(b, rb):
            start = jnp.clip(rpb * rb - NA_KH // 2, 0, rows - 2 * rpb)
            return (pl.multiple_of((b * rows + start) * GRID_W, GRID_W), col)
        return index

    keys = NA_KH * GRID_W
    vmem = (2 * (tq * width * 2 + 2 * band * width * 2 + tq * width * 2) + 2 * bias_t.size * 4 + GRID_W * width * 4
            + n_h * GRID_W * keys * 6)
    return pl.pallas_call(
        functools.partial(_attn_b_kernel, rows=rows, n_h=n_h, rpb=rpb),
        grid=(bsz, nrb),
        in_specs=[
            pl.BlockSpec((pl.Element(tq), pl.Element(width)),
                         lambda b, rb: (pl.multiple_of((b * nrb + rb) * tq, tq), q_col)),
            pl.BlockSpec((pl.Element(band), pl.Element(width)), band_map(k_col)),
            pl.BlockSpec((pl.Element(band), pl.Element(width)), band_map(v_col)),
            pl.BlockSpec(bias_t.shape, lambda b, rb: (0, 0, 0)),
            pl.BlockSpec((1, width), lambda b, rb: (0, 0)),
        ],
        out_specs=pl.BlockSpec((tq, width), lambda b, rb: (b * nrb + rb, 0)),
        out_shape=jax.ShapeDtypeStruct((m, width), BF16),
        scratch_shapes=[pltpu.VMEM((GRID_W, width), F32), pltpu.VMEM((n_h * GRID_W, keys), F32),
                        pltpu.VMEM((n_h * GRID_W, keys), BF16)],
        compiler_params=pltpu.CompilerParams(
            dimension_semantics=("parallel", "arbitrary"), vmem_limit_bytes=_vmem_limit(vmem)),
        name="attn_b",
    )(proj, proj, proj, bias_t, gain)


def _out_proj_kernel(a_ref, b_ref, wa_ref, wb_ref, x_ref, o_ref):
    o_ref[...] = x_ref[...] + (_dot(a_ref[...], wa_ref[...]) + _dot(b_ref[...], wb_ref[...]))


def _out_proj(oa, ob, w, x, *, tm=1024, tn=512):
    m, d = x.shape
    ka, kb = oa.shape[1], ob.shape[1]
    tm = min(tm, m)
    assert ka == kb and m % tm == 0 and d % tn == 0
    vmem = 2 * (2 * tm * ka * 2 + 2 * ka * tn * 2 + 2 * tm * tn * 4)
    return pl.pallas_call(
        _out_proj_kernel,
        grid=(m // tm, d // tn),
        in_specs=[
            pl.BlockSpec((tm, ka), lambda i, j: (i, 0)),
            pl.BlockSpec((tm, kb), lambda i, j: (i, 0)),
            pl.BlockSpec((ka, tn), lambda i, j: (0, j)),
            pl.BlockSpec((kb, tn), lambda i, j: (1, j)),
            pl.BlockSpec((tm, tn), lambda i, j: (i, j)),
        ],
        out_specs=pl.BlockSpec((tm, tn), lambda i, j: (i, j)),
        out_shape=jax.ShapeDtypeStruct((m, d), F32),
        compiler_params=pltpu.CompilerParams(
            dimension_semantics=("parallel", "arbitrary"), vmem_limit_bytes=_vmem_limit(vmem)),
        name="out_proj",
    )(oa, ob, w, w, x)


def _mem_kv_kernel(mem_ref, gain_ref, w_ref, o_ref, h_ref, *, tm):
    _norm_rows(mem_ref, gain_ref, h_ref, tm)
    o_ref[...] = _dot(h_ref[...], w_ref[...]).astype(o_ref.dtype)


def _mem_kv(mem, gain, w, *, tm=256):
    m, d = mem.shape
    n = w.shape[1]
    tm = min(tm, m)
    assert m % tm == 0
    vmem = 2 * tm * d * 4 + tm * d * 2 + 2 * d * n * 2 + 2 * tm * n * 2
    return pl.pallas_call(
        functools.partial(_mem_kv_kernel, tm=tm),
        grid=(m // tm,),
        in_specs=[
            pl.BlockSpec((tm, d), lambda i: (i, 0)),
            pl.BlockSpec((1, d), lambda i: (0, 0)),
            pl.BlockSpec((d, n), lambda i: (0, 0)),
        ],
        out_specs=pl.BlockSpec((tm, n), lambda i: (i, 0)),
        out_shape=jax.ShapeDtypeStruct((m, n), BF16),
        scratch_shapes=[pltpu.VMEM((tm, d), BF16)],
        compiler_params=pltpu.CompilerParams(
            dimension_semantics=("parallel",), vmem_limit_bytes=_vmem_limit(vmem)),
        name="mem_kv",
    )(mem, gain, w)


def _cross_kernel(x_ref, gain_ref, wq_ref, k_ref, v_ref, wo_ref, o_ref, h_ref, att_ref, *, tm):
    gain = gain_ref[...]
    rt = tm // CROSS_ROW_SPLIT
    for r0 in range(0, tm, rt):
        for g0 in range(r0, r0 + rt, NORM_ROWS):
            h_ref[g0:g0 + NORM_ROWS, :] = _rms(x_ref[g0:g0 + NORM_ROWS, :], gain).astype(h_ref.dtype)
        q = _dot(h_ref[r0:r0 + rt, :], wq_ref[...]).astype(BF16)
        for h in range(CA_HEADS):
            hs = slice(h * HEAD_DIM, (h + 1) * HEAD_DIM)
            s = _dot_nt(q[:, hs], k_ref[:, hs]) * (SCALE * LOG2E)
            mx = jnp.max(s, axis=-1, keepdims=True)
            p = jnp.exp2(s - mx)
            den = jnp.sum(p, axis=-1, keepdims=True)
            att_ref[r0:r0 + rt, hs] = (_dot(p.astype(BF16), v_ref[:, hs]) / den).astype(BF16)
        o_ref[r0:r0 + rt, :] = x_ref[r0:r0 + rt, :] + _dot(att_ref[r0:r0 + rt, :], wo_ref[...])


def _cross(x, gain, wq, kv, wo, *, s_len, n_mem, tm=512):
    m, d = x.shape
    caw = wq.shape[1]
    tm = min(tm, s_len)
    assert m % tm == 0 and s_len % tm == 0 and caw == CA_HEADS * HEAD_DIM
    per_seq = s_len // tm
    vmem = 4 * tm * d * 4 + tm * d * 2 + 2 * 2 * d * caw * 2 + 4 * n_mem * caw * 2 + tm * caw * 2
    return pl.pallas_call(
        functools.partial(_cross_kernel, tm=tm),
        grid=(m // tm,),
        in_specs=[
            pl.BlockSpec((tm, d), lambda i: (i, 0)),
            pl.BlockSpec((1, d), lambda i: (0, 0)),
            pl.BlockSpec((d, caw), lambda i: (0, 0)),
            pl.BlockSpec((n_mem, caw), lambda i: (i // per_seq, 0)),
            pl.BlockSpec((n_mem, caw), lambda i: (i // per_seq, 1)),
            pl.BlockSpec((caw, d), lambda i: (0, 0)),
        ],
        out_specs=pl.BlockSpec((tm, d), lambda i: (i, 0)),
        out_shape=jax.ShapeDtypeStruct((m, d), F32),
        scratch_shapes=[pltpu.VMEM((tm, d), BF16), pltpu.VMEM((tm, caw), BF16)],
        compiler_params=pltpu.CompilerParams(
            dimension_semantics=("parallel",), vmem_limit_bytes=_vmem_limit(vmem)),
        name="cross",
    )(x, gain, wq, kv, kv, wo)


def _trunk(x3, mem3, p, late=None):
    bsz, s_len, d = x3.shape
    n_mem = mem3.shape[1]
    x = x3.reshape(bsz * s_len, d)
    mem = mem3.reshape(bsz * n_mem, d)
    cos_t, sin_t = _rope_tables(s_len)

    ffn1 = functools.partial(_ffn, x, p["ffn1_norm"], p["ffn1_w_gate"], p["ffn1_w_up"], p["ffn1_w_down"],
                             p["final_norm"], final=False)
    if late is None:
        x, *cast = ffn1(cast=tuple(p["late_f32"].values()))
        late = dict(zip(p["late_f32"], cast))
    else:
        x = ffn1()
    proj = _proj_in(x, p["mix_norm"], late["w_in"], cos_t, sin_t, s_len=s_len, rope_ranges=p["rope_ranges"])
    c = p["cols"]
    oa = _attn_a(proj, p["a_sink"], p["a_out_norm"], bsz=bsz, s_len=s_len,
                 q_col=c["qa"], k_col=c["ka"], v_col=c["va"], n_q=p["a_q_heads"], n_kv=p["a_kv_heads"])
    ob = _attn_b(proj, p["b_bias"], p["b_out_norm"], bsz=bsz, s_len=s_len,
                 q_col=c["qb"], k_col=c["kb"], v_col=c["vb"], n_h=p["b_heads"])
    x = _out_proj(oa, ob, late["w_out"], x)
    kv = _mem_kv(mem, p["mem_norm"], p["ca_w_kv"])
    x = _cross(x, p["ca_norm"], p["ca_w_q"], kv, p["ca_w_o"], s_len=s_len, n_mem=n_mem)
    y = _ffn(x, p["ffn2_norm"], late["ffn2_w_gate"], late["ffn2_w_up"], late["ffn2_w_down"], p["final_norm"],
             final=True)
    return y.reshape(bsz, s_len, d), late


def _prepare(ffn1_norm, ffn1_w_gate, ffn1_w_up, ffn1_w_down, mix_norm, w_in, a_sink, b_rel_bias, a_out_norm,
             b_out_norm, w_out, ca_norm, mem_norm, ca_w_q, ca_w_kv, ca_w_o, ffn2_norm, ffn2_w_gate, ffn2_w_up,
             ffn2_w_down, final_norm):
    d = w_in.shape[1]
    n_heads = d // HEAD_DIM
    a_q = n_heads // 2
    a_kv = a_q // 4
    b_h = n_heads - a_q
    aq_w, akv_w, b_w = a_q * HEAD_DIM, a_kv * HEAD_DIM, b_h * HEAD_DIM
    cols = {"qa": 0, "ka": aq_w, "va": aq_w + akv_w, "qb": aq_w + 2 * akv_w, "kb": aq_w + 2 * akv_w + b_w,
            "vb": aq_w + 2 * akv_w + 2 * b_w}

    def gain(g):
        return g.reshape(1, -1).astype(F32)

    def wt(a):
        return a[0].astype(BF16)

    return {
        "ffn1_norm": gain(ffn1_norm[0]), "ffn1_w_gate": wt(ffn1_w_gate), "ffn1_w_up": wt(ffn1_w_up),
        "ffn1_w_down": wt(ffn1_w_down),
        "mix_norm": gain(mix_norm[0]), "cols": cols,
        "rope_ranges": ((cols["qa"], cols["qa"] + aq_w), (cols["ka"], cols["ka"] + akv_w)),
        "a_sink": a_sink[0].astype(F32), "b_bias": _bias_tables(b_rel_bias[0]),
        "a_out_norm": gain(a_out_norm[0]), "b_out_norm": gain(b_out_norm[0]),
        "a_q_heads": a_q, "a_kv_heads": a_kv, "b_heads": b_h,
        "ca_norm": gain(ca_norm[0]), "mem_norm": gain(mem_norm[0]), "ca_w_q": wt(ca_w_q), "ca_w_kv": wt(ca_w_kv),
        "ca_w_o": wt(ca_w_o),
        "ffn2_norm": gain(ffn2_norm[0]), "final_norm": gain(final_norm),
        "late_f32": {"w_in": w_in[0], "w_out": w_out[0], "ffn2_w_gate": ffn2_w_gate[0], "ffn2_w_up": ffn2_w_up[0],
                     "ffn2_w_down": ffn2_w_down[0]},
    }


def kernel(x_prompt, x_sample, mem_prompt, mem_sample, ffn1_norm, ffn1_w_gate, ffn1_w_up, ffn1_w_down, mix_norm, w_in, a_sink, b_rel_bias, a_out_norm, b_out_norm, w_out, ca_norm, mem_norm, ca_w_q, ca_w_kv, ca_w_o, ffn2_norm, ffn2_w_gate, ffn2_w_up, ffn2_w_down, final_norm):
    assert ffn1_w_gate.shape[0] == 1, "single-layer trunk"
    p = _prepare(ffn1_norm, ffn1_w_gate, ffn1_w_up, ffn1_w_down, mix_norm, w_in, a_sink, b_rel_bias, a_out_norm,
                 b_out_norm, w_out, ca_norm, mem_norm, ca_w_q, ca_w_kv, ca_w_o, ffn2_norm, ffn2_w_gate, ffn2_w_up,
                 ffn2_w_down, final_norm)
    y_prompt, late = _trunk(x_prompt, mem_prompt, p)
    y_sample, _ = _trunk(x_sample, mem_sample, p, late)
    return (y_prompt, y_sample)
```
